```python
import math
import jax
import jax.numpy as jnp
from jax import lax
import numpy as np

D_MODEL = 1024
BATCH = 1
SEQ = 16384
DEPTH = 1
DEC_BATCH = 8
DEC_SEQ = 2048
PAST_LEN = 128

DA_HEADS = 4
DA_HEAD_DIM = 64
DA_V_DIM = 2 * DA_HEAD_DIM
DA_WIDTH = DA_HEADS * DA_V_DIM
RW_WIDTH = D_MODEL - DA_WIDTH
RW_HEAD = 64
RW_HEADS = RW_WIDTH // RW_HEAD
DECAY_LORA = 64
ICLR_LORA = 64
GATE_LORA = 128
D_FF = 4 * D_MODEL
ROPE_THETA = 10000.0
Q_BLOCK = 128
NORM_EPS = 1e-6
LN_X_EPS = 64e-5
DA_QK_COLS = DA_HEADS * 2 * DA_HEAD_DIM
DA_COLS = 2 * DA_QK_COLS + DA_WIDTH
RW_COLS = 3 * RW_WIDTH + DECAY_LORA + ICLR_LORA + GATE_LORA
IN_COLS = DA_COLS + RW_COLS

kernel_name = "hymba_diffattn_birwkv7_encoder"


def rms_norm(x, g, eps=NORM_EPS):
    xf = x.astype(jnp.float32)
    y = xf * lax.rsqrt(jnp.mean(xf * xf, axis=-1, keepdims=True) + eps)
    return (y * g.astype(jnp.float32)).astype(x.dtype)


def rope_tables(seq_len):
    inv_freq = 1.0 / (ROPE_THETA ** (jnp.arange(0, DA_HEAD_DIM, 2, dtype=jnp.float32) / DA_HEAD_DIM))
    ang = jnp.arange(seq_len, dtype=jnp.float32)[:, None] * inv_freq[None, :]
    ang = jnp.concatenate([ang, ang], axis=-1)
    return jnp.cos(ang), jnp.sin(ang)


def apply_rope(x, cos, sin):
    half = DA_HEAD_DIM // 2
    xf = x.astype(jnp.float32)
    rot = jnp.concatenate([-xf[..., half:], xf[..., :half]], axis=-1)
    c = cos[None, :, None, None, :]
    s = sin[None, :, None, None, :]
    return (xf * c + rot * s).astype(x.dtype)


def diff_attention(q, k, v, lam):
    B, S = q.shape[0], q.shape[1]
    nb = S // Q_BLOCK
    scale = DA_HEAD_DIM ** -0.5
    qb = jnp.moveaxis(q.reshape(B, nb, Q_BLOCK, DA_HEADS, 2, DA_HEAD_DIM), 1, 0)

    def one_block(q_blk):
        s = jnp.einsum("bqhcd,bkhcd->bhcqk", q_blk, k, preferred_element_type=jnp.float32) * scale
        p = jax.nn.softmax(s, axis=-1)
        attn = p[:, :, 0] - lam * p[:, :, 1]
        return jnp.einsum("bhqk,bkhe->bqhe", attn.astype(v.dtype), v)

    o = lax.map(one_block, qb)
    return jnp.moveaxis(o, 0, 1).reshape(B, S, DA_HEADS, DA_V_DIM)


def diff_attn_mixer(z, cos, sin, lambda_init, q_norm_g, k_norm_g, lam_q1, lam_k1, lam_q2, lam_k2, subln_g):
    B, S = z.shape[0], z.shape[1]
    q = z[..., :DA_QK_COLS].reshape(B, S, DA_HEADS, 2, DA_HEAD_DIM)
    k = z[..., DA_QK_COLS:2 * DA_QK_COLS].reshape(B, S, DA_HEADS, 2, DA_HEAD_DIM)
    v = z[..., 2 * DA_QK_COLS:].reshape(B, S, DA_HEADS, DA_V_DIM)
    q = apply_rope(rms_norm(q, q_norm_g), cos, sin)
    k = apply_rope(rms_norm(k, k_norm_g), cos, sin)
    lam = (jnp.exp(jnp.sum(lam_q1.astype(jnp.float32) * lam_k1.astype(jnp.float32)))
           - jnp.exp(jnp.sum(lam_q2.astype(jnp.float32) * lam_k2.astype(jnp.float32)))
           + lambda_init)
    o = diff_attention(q, k, v, lam)
    o = rms_norm(o, subln_g) * (1.0 - lambda_init)
    return o.reshape(B, S, DA_WIDTH)


def rwkv7_scan(r, w, k, v, a, b):
    def step(state, inp):
        r_t, w_t, k_t, v_t, a_t, b_t = inp
        sa = jnp.einsum("dbhij,dbhj->dbhi", state, a_t)
        state = (state * w_t[..., None, :] + sa[..., :, None] * b_t[..., None, :]
                 + v_t[..., :, None] * k_t[..., None, :])
        y = jnp.einsum("dbhij,dbhj->dbhi", state, r_t)
        return state, y

    s0 = jnp.zeros(r.shape[1:] + (RW_HEAD,), jnp.float32)
    _, y = lax.scan(step, s0, (r, w, k, v, a, b))
    return y


def rwkv7_mixer(z, mu_prev, mu_next, w0, w_up, a0, a_up, g_up, k_k, k_a, r_k, ln_x_g, ln_x_b):
    B, S = z.shape[0], z.shape[1]
    zf = z.astype(jnp.float32)
    z_prev = jnp.pad(zf[:, :-1], ((0, 0), (1, 0), (0, 0)))
    z_next = jnp.pad(zf[:, 1:], ((0, 0), (0, 1), (0, 0)))
    zf = zf + mu_prev * (z_prev - zf) + mu_next * (z_next - zf)
    o1, o2, o3 = RW_WIDTH, 2 * RW_WIDTH, 3 * RW_WIDTH
    o4 = o3 + DECAY_LORA
    o5 = o4 + ICLR_LORA
    r, k, v = zf[..., :o1], zf[..., o1:o2], zf[..., o2:o3]
    w_dn, a_dn, g_dn = zf[..., o3:o4], zf[..., o4:o5], zf[..., o5:]
    w_log = -jax.nn.softplus(-(w0[:, None, None, :] + jnp.einsum("bsr,drc->dbsc", jnp.tanh(w_dn), w_up))) - 0.5
    decay = jnp.exp(-jnp.exp(w_log))
    a_rate = jax.nn.sigmoid(a0[:, None, None, :] + jnp.einsum("bsr,drc->dbsc", a_dn, a_up))
    g = jnp.einsum("bsr,rc->bsc", jax.nn.sigmoid(g_dn), g_up)

    def heads(t):
        return t.reshape(t.shape[:-1] + (RW_HEADS, RW_HEAD))

    kk = heads(k * k_k)
    kk = kk * lax.rsqrt(jnp.sum(kk * kk, axis=-1, keepdims=True) + 1e-12)
    k_dir = heads(k[None] * (1.0 + (a_rate - 1.0) * k_a))
    a_dir = heads(a_rate)
    decay = heads(decay)
    r_h, v_h = heads(r), heads(v)

    def both(t):
        return jnp.broadcast_to(t[None], (2,) + t.shape)

    def time_major(t):
        t = jnp.stack([t[0], jnp.flip(t[1], axis=1)])
        return jnp.moveaxis(t, 2, 0)

    y = rwkv7_scan(time_major(both(r_h)), time_major(decay), time_major(k_dir),
                   time_major(both(v_h)), time_major(both(-kk)), time_major(kk[None] * a_dir))
    y = jnp.moveaxis(y, 0, 2)
    y = y[0] + jnp.flip(y[1], axis=1)
    mean = jnp.mean(y, axis=-1, keepdims=True)
    var = jnp.mean(jnp.square(y - mean), axis=-1, keepdims=True)
    y = ((y - mean) * lax.rsqrt(var + LN_X_EPS)).reshape(B, S, RW_WIDTH) * ln_x_g + ln_x_b
    bonus = jnp.sum(r_h[None] * k_dir * r_k, axis=-1, keepdims=True) * v_h[None]
    y = y + jnp.sum(bonus, axis=0).reshape(B, S, RW_WIDTH)
    return (y * g).astype(z.dtype)


def encoder_layer(x, cos, sin, lambda_init, norm1_g, w_in, q_norm_g, k_norm_g, lam_q1, lam_k1,
                  lam_q2, lam_k2, subln_g, mu_prev, mu_next, w0, w_up, a0, a_up, g_up, k_k, k_a,
                  r_k, ln_x_g, ln_x_b, w_out, norm2_g, w_ff1, w_ff2):
    h = rms_norm(x, norm1_g)
    z = jnp.einsum("bsd,dc->bsc", h, w_in)
    o_da = diff_attn_mixer(z[..., :DA_COLS], cos, sin, lambda_init, q_norm_g, k_norm_g,
                           lam_q1, lam_k1, lam_q2, lam_k2, subln_g)
    o_rw = rwkv7_mixer(z[..., DA_COLS:], mu_prev, mu_next, w0, w_up, a0, a_up, g_up,
                       k_k, k_a, r_k, ln_x_g, ln_x_b)
    mixed = jnp.concatenate([o_da, o_rw.astype(o_da.dtype)], axis=-1)
    x = x + jnp.einsum("bsc,cd->bsd", mixed, w_out)
    h2 = rms_norm(x, norm2_g)
    u = jax.nn.relu(jnp.einsum("bsd,df->bsf", h2, w_ff1))
    return x + jnp.einsum("bsf,fd->bsd", u * u, w_ff2)


def setup_inputs(seed: int = 0) -> dict:
    key = jax.random.key(seed)
    ks = jax.random.split(key, 32)
    f32 = jnp.float32
    nrm = lambda i, shape: jax.random.normal(ks[i], shape, f32)
    L = DEPTH
    return {
        "x_prompt": nrm(0, (BATCH, SEQ, D_MODEL)),
        "x_sample": nrm(1, (DEC_BATCH, DEC_SEQ, D_MODEL)),
        "norm1_g": 1.0 + 0.02 * nrm(2, (L, D_MODEL)),
        "w_in": nrm(3, (L, D_MODEL, IN_COLS)) * D_MODEL ** -0.5,
        "q_norm_g": 1.0 + 0.02 * nrm(4, (L, DA_HEAD_DIM)),
        "k_norm_g": 1.0 + 0.02 * nrm(5, (L, DA_HEAD_DIM)),
        "lam_q1": 0.1 * nrm(6, (L, DA_HEAD_DIM)),
        "lam_k1": 0.1 * nrm(7, (L, DA_HEAD_DIM)),
        "lam_q2": 0.1 * nrm(8, (L, DA_HEAD_DIM)),
        "lam_k2": 0.1 * nrm(9, (L, DA_HEAD_DIM)),
        "subln_g": 1.0 + 0.02 * nrm(10, (L, DA_V_DIM)),
        "mu_prev": jax.random.uniform(ks[11], (L, RW_COLS), f32, 0.0, 0.5),
        "mu_next": jax.random.uniform(ks[12], (L, RW_COLS), f32, 0.0, 0.5),
        "w0": jax.random.uniform(ks[13], (L, 2, RW_WIDTH), f32, -4.0, 0.0),
        "w_up": 0.5 * nrm(14, (L, 2, DECAY_LORA, RW_WIDTH)) * DECAY_LORA ** -0.5,
        "a0": 0.1 * nrm(15, (L, 2, RW_WIDTH)),
        "a_up": 0.5 * nrm(16, (L, 2, ICLR_LORA, RW_WIDTH)) * ICLR_LORA ** -0.5,
        "g_up": nrm(17, (L, GATE_LORA, RW_WIDTH)) * GATE_LORA ** -0.5,
        "k_k": 0.85 + 0.05 * nrm(18, (L, RW_WIDTH)),
        "k_a": 1.0 + 0.05 * nrm(19, (L, RW_WIDTH)),
        "r_k": 0.1 * nrm(20, (L, RW_HEADS, RW_HEAD)),
        "ln_x_g": 1.0 + 0.02 * nrm(21, (L, RW_WIDTH)),
        "ln_x_b": 0.02 * nrm(22, (L, RW_WIDTH)),
        "w_out": nrm(23, (L, D_MODEL, D_MODEL)) * D_MODEL ** -0.5,
        "norm2_g": 1.0 + 0.02 * nrm(24, (L, D_MODEL)),
        "w_ff1": nrm(25, (L, D_MODEL, D_FF)) * D_MODEL ** -0.5,
        "w_ff2": nrm(26, (L, D_FF, D_MODEL)) * D_FF ** -0.5,
    }


def reference(x_prompt, x_sample, norm1_g, w_in, q_norm_g, k_norm_g, lam_q1, lam_k1, lam_q2,
              lam_k2, subln_g, mu_prev, mu_next, w0, w_up, a0, a_up, g_up, k_k, k_a, r_k,
              ln_x_g, ln_x_b, w_out, norm2_g, w_ff1, w_ff2):
    def run(x):
        cos, sin = rope_tables(x.shape[1])
        for l in range(DEPTH):
            lambda_init = 0.8 - 0.6 * math.exp(-0.3 * l)
            x = encoder_layer(x, cos, sin, lambda_init, norm1_g[l], w_in[l], q_norm_g[l],
                              k_norm_g[l], lam_q1[l], lam_k1[l], lam_q2[l], lam_k2[l],
                              subln_g[l], mu_prev[l], mu_next[l], w0[l], w_up[l], a0[l],
                              a_up[l], g_up[l], k_k[l], k_a[l], r_k[l], ln_x_g[l], ln_x_b[l],
                              w_out[l], norm2_g[l], w_ff1[l], w_ff2[l])
        return x

    y_prompt = run(x_prompt)
    y_sample = run(x_sample)
    return (y_prompt, y_sample)
```

```python
import functools
import math

import jax
import jax.numpy as jnp
from jax import lax
from jax.experimental import pallas as pl
from jax.experimental.pallas import tpu as pltpu

F32 = jnp.float32
BF16 = jnp.bfloat16

D_MODEL = 1024
DA_HEADS = 4
HEAD_DIM = 64
DA_V_DIM = 128
DA_WIDTH = DA_HEADS * DA_V_DIM
RW_WIDTH = D_MODEL - DA_WIDTH
RW_HEADS = RW_WIDTH // HEAD_DIM
DECAY_LORA = 64
ICLR_LORA = 64
GATE_LORA = 128
LORA_IN = DECAY_LORA + ICLR_LORA
RW_COLS = 3 * RW_WIDTH + LORA_IN + GATE_LORA
DA_COLS = 3 * DA_WIDTH
IN_COLS = DA_COLS + RW_COLS
D_FF = 4 * D_MODEL
ROPE_THETA = 10000.0
NORM_EPS = 1e-6
LN_X_EPS = 64e-5
KK_EPS = 1e-12
LAMBDA_INIT = 0.8 - 0.6 * math.exp(-0.3 * 0)
LOG2E = 1.4426950408889634
QK_SCALE = HEAD_DIM ** -0.5

LANES = 128
CHUNK = 64
VMEM_LIMIT = 56 * 1024 * 1024

PV_W0, PV_A0, PV_KK, PV_KA, PV_RK, PV_LNG, PV_LNB = 0, 2, 4, 5, 6, 7, 8
PV_ROWS = 16


def _dot(a, b):
    return jnp.dot(a, b, preferred_element_type=F32)


def _split2(x):
    hi = x.astype(BF16)
    lo = (x - hi.astype(F32)).astype(BF16)
    return hi, lo


def _split3(x):
    hi = x.astype(BF16)
    r1 = x - hi.astype(F32)
    mid = r1.astype(BF16)
    lo = (r1 - mid.astype(F32)).astype(BF16)
    return hi, mid, lo


def _seg_sum(x, seg):
    hi, lo = _split2(x)
    return _dot(hi, seg) + _dot(lo, seg)


def _in_proj_kernel(x_ref, g1_ref, w_ref, qg_ref, kg_ref, cos_ref, sin_ref, seg_ref,
                    q_ref, k_ref, v_ref, z_ref):
    x = x_ref[0]
    ms = jnp.mean(x * x, axis=-1, keepdims=True)
    h = (x * lax.rsqrt(ms + NORM_EPS) * g1_ref[...]).astype(BF16)
    seg = seg_ref[...]
    reps = DA_WIDTH // LANES
    cos = jnp.concatenate([cos_ref[...]] * reps, axis=1)
    sin = jnp.concatenate([sin_ref[...]] * reps, axis=1)
    lane = lax.broadcasted_iota(jnp.int32, (1, LANES), 1)
    first_half = (lane % HEAD_DIM) < (HEAD_DIM // 2)

    def head_norm_rope(z, g):
        ss = _seg_sum(z * z, seg) * (1.0 / HEAD_DIM)
        zn = z * lax.rsqrt(ss + NORM_EPS) * g
        parts = []
        for c in range(reps):
            zc = zn[:, c * LANES:(c + 1) * LANES]
            parts.append(jnp.where(first_half,
                                   pltpu.roll(zc, LANES - HEAD_DIM // 2, 1),
                                   pltpu.roll(zc, HEAD_DIM // 2, 1)))
        rot = jnp.concatenate(parts, axis=1)
        return zn * cos + rot * sin

    zq = _dot(h, w_ref[:, 0:DA_WIDTH])
    q_ref[0] = (head_norm_rope(zq, qg_ref[...]) * (QK_SCALE * LOG2E)).astype(BF16)
    zk = _dot(h, w_ref[:, DA_WIDTH:2 * DA_WIDTH])
    k_ref[0] = head_norm_rope(zk, kg_ref[...]).astype(BF16)
    v_ref[0] = _dot(h, w_ref[:, 2 * DA_WIDTH:DA_COLS]).astype(BF16)
    z_ref[0] = _dot(h, w_ref[:, DA_COLS:IN_COLS])


def _in_proj(x, g1, w_in, qg, kg, cos_t, sin_t, seg, tm):
    B, S, _ = x.shape
    const = lambda shape: pl.BlockSpec(shape, lambda b, i: (0,) * len(shape))
    tok = lambda width: pl.BlockSpec((1, tm, width), lambda b, i: (b, i, 0))
    return pl.pallas_call(
        _in_proj_kernel,
        grid=(B, S // tm),
        in_specs=[tok(D_MODEL), const((1, D_MODEL)), const((D_MODEL, IN_COLS)),
                  const((1, DA_WIDTH)), const((1, DA_WIDTH)),
                  pl.BlockSpec((tm, LANES), lambda b, i: (i, 0)),
                  pl.BlockSpec((tm, LANES), lambda b, i: (i, 0)),
                  const((DA_WIDTH, DA_WIDTH))],
        out_specs=[tok(DA_WIDTH), tok(DA_WIDTH), tok(DA_WIDTH), tok(RW_COLS)],
        out_shape=[jax.ShapeDtypeStruct((B, S, DA_WIDTH), BF16)] * 3
        + [jax.ShapeDtypeStruct((B, S, RW_COLS), F32)],
        compiler_params=pltpu.CompilerParams(
            dimension_semantics=("parallel", "parallel"), vmem_limit_bytes=VMEM_LIMIT),
        name="in_proj",
    )(x, g1, w_in, qg, kg, cos_t, sin_t, seg)


def _diff_attn_kernel(q_ref, k_ref, v_ref, lam_ref, sg_ref, o_ref, *, tk):
    q = q_ref[0]
    tq = q.shape[0]
    nk = k_ref.shape[1] // tk
    lane = lax.broadcasted_iota(jnp.int32, (1, LANES), 1)
    zero = jnp.zeros_like(q)
    qs = (jnp.where(lane < HEAD_DIM, q, zero), jnp.where(lane >= HEAD_DIM, q, zero))

    def body(j, carry):
        start = pl.multiple_of(j * tk, tk)
        kb = k_ref[0, pl.ds(start, tk), :]
        vb = v_ref[0, pl.ds(start, tk), :]
        out = []
        for c in range(2):
            m, l, acc = carry[c]
            s = lax.dot_general(qs[c], kb, (((1,), (1,)), ((), ())), preferred_element_type=F32)
            m_new = jnp.maximum(m, jnp.max(s, axis=1, keepdims=True))
            alpha = jnp.exp2(m - m_new)
            p = jnp.exp2(s - m_new)
            l = alpha * l + jnp.sum(p, axis=1, keepdims=True)
            acc = alpha * acc + _dot(p.astype(BF16), vb)
            out.append((m_new, l, acc))
        return tuple(out)

    init = tuple((jnp.full((tq, 1), -1e30, F32), jnp.zeros((tq, 1), F32),
                  jnp.zeros((tq, DA_V_DIM), F32)) for _ in range(2))
    (_, l0, acc0), (_, l1, acc1) = lax.fori_loop(0, nk, body, init)

    lp = lam_ref[...]
    lam = (jnp.exp(jnp.sum(lp[0:1] * lp[1:2], axis=1, keepdims=True))
           - jnp.exp(jnp.sum(lp[2:3] * lp[3:4], axis=1, keepdims=True)) + LAMBDA_INIT)
    o = acc0 / l0 - lam * (acc1 / l1)
    ms = jnp.mean(o * o, axis=-1, keepdims=True)
    o_ref[0] = o * lax.rsqrt(ms + NORM_EPS) * sg_ref[...] * (1.0 - LAMBDA_INIT)


def _diff_attn(q, k, v, lam_p, subln_g, tq, tk):
    B, S, _ = q.shape
    return pl.pallas_call(
        functools.partial(_diff_attn_kernel, tk=tk),
        grid=(B, DA_HEADS, S // tq),
        in_specs=[pl.BlockSpec((1, tq, LANES), lambda b, h, i: (b, i, h)),
                  pl.BlockSpec((1, S, LANES), lambda b, h, i: (b, 0, h)),
                  pl.BlockSpec((1, S, LANES), lambda b, h, i: (b, 0, h)),
                  pl.BlockSpec((4, HEAD_DIM), lambda b, h, i: (0, 0)),
                  pl.BlockSpec((1, DA_V_DIM), lambda b, h, i: (0, 0))],
        out_specs=pl.BlockSpec((1, tq, DA_V_DIM), lambda b, h, i: (b, i, h)),
        out_shape=jax.ShapeDtypeStruct((B, S, DA_WIDTH), F32),
        compiler_params=pltpu.CompilerParams(
            dimension_semantics=("parallel", "parallel", "parallel"), vmem_limit_bytes=VMEM_LIMIT),
        name="diff_attn",
    )(q, k, v, lam_p, subln_g)


def _heads(x):
    return jnp.stack([x[:, h * HEAD_DIM:(h + 1) * HEAD_DIM] for h in range(RW_HEADS)])


def _bmm(a, b):
    return jnp.einsum("hts,hsu->htu", a.astype(BF16), b.astype(BF16), preferred_element_type=F32)


def _bmm_nt(a, b):
    return jnp.einsum("htj,hsj->hts", a.astype(BF16), b.astype(BF16), preferred_element_type=F32)


def _bmm_tn(a, b):
    return jnp.einsum("hsi,hsj->hij", a.astype(BF16), b.astype(BF16), preferred_element_type=F32)


def _rwkv_prep(d, z_ref, zp_ref, zn_ref, tile, n_tiles, mu_ref, pv_ref, wup_ref, aup_ref, gup_ref, seg):
    z = z_ref[0]
    tm = z.shape[0]
    prev_row = jnp.where(tile > 0, zp_ref[0, 7:8, :], 0.0)
    next_row = jnp.where(tile < n_tiles - 1, zn_ref[0, 0:1, :], 0.0)
    row = lax.broadcasted_iota(jnp.int32, (tm, 1), 0)
    z_prev = jnp.where(row == 0, prev_row, pltpu.roll(z, 1, 0))
    z_next = jnp.where(row == tm - 1, next_row, pltpu.roll(z, tm - 1, 0))
    zs = z + mu_ref[0:1] * (z_prev - z) + mu_ref[1:2] * (z_next - z)

    w = RW_WIDTH
    r, k, v = zs[:, 0:w], zs[:, w:2 * w], zs[:, 2 * w:3 * w]
    wa = zs[:, 3 * w:3 * w + LORA_IN]
    pv = lambda i: pv_ref[i:i + 1]
    w_pre = _dot(jnp.tanh(wa).astype(BF16), wup_ref[d])
    a_pre = _dot(wa.astype(BF16), aup_ref[d])
    u = -(pv(PV_W0 + d) + w_pre)
    softplus = jnp.maximum(u, 0.0) + jnp.log1p(jnp.exp(-jnp.abs(u)))
    w_log = -softplus - 0.5
    lw = -jnp.exp(w_log)
    a_rate = jax.nn.sigmoid(pv(PV_A0 + d) + a_pre)
    kk = k * pv(PV_KK)
    kk = kk * lax.rsqrt(_seg_sum(kk * kk, seg) + KK_EPS)
    kd = k * (1.0 + (a_rate - 1.0) * pv(PV_KA))
    bonus = _seg_sum(r * kd * pv(PV_RK), seg) * v
    out = dict(r=r, v=v, kd=kd, lw=lw, a=-kk, b=kk * a_rate, bonus=bonus)
    if d == 0:
        g_dn = zs[:, 3 * w + LORA_IN:RW_COLS]
        out["g"] = _dot(jax.nn.sigmoid(g_dn).astype(BF16), gup_ref[...])
    return out


def _rwkv_chunk(d, ci, ops_ref, tri_ref, st_ref, y_ref):
    rows = pl.ds(pl.multiple_of(ci * CHUNK, CHUNK), CHUNK)
    r, v, kd, lw, a, b = (ops_ref[d, n, rows, :] for n in range(6))
    tri = tri_ref[d]
    cl = sum(_dot(tri, part) for part in _split3(lw))
    tot = cl[CHUNK - 1:CHUNK] if d == 0 else cl[0:1]
    e_inv = jnp.exp(-cl)
    e_rem = jnp.exp(tot - cl)
    at = _heads(a * jnp.exp(cl - lw))
    rt = _heads(r * jnp.exp(cl))
    bt = _heads(b * e_inv)
    kt = _heads(kd * e_inv)
    bh = _heads(b * e_rem)
    kh = _heads(kd * e_rem)
    vh = _heads(v)
    p_tot = _heads(jnp.exp(tot))

    ti = lax.broadcasted_iota(jnp.int32, (CHUNK, CHUNK), 0)
    si = lax.broadcasted_iota(jnp.int32, (CHUNK, CHUNK), 1)
    strict = (si < ti) if d == 0 else (si > ti)
    incl = (si <= ti) if d == 0 else (si >= ti)
    eye = (si == ti).astype(F32)

    ar = jnp.concatenate([at, rt], axis=1)
    ab = _bmm_nt(ar, bt)
    ak = _bmm_nt(ar, kt)
    l_mat = jnp.where(strict, ab[:, :CHUNK], 0.0)
    a_ak = jnp.where(strict, ak[:, :CHUNK], 0.0)
    a_rb = jnp.where(incl, ab[:, CHUNK:], 0.0)
    a_rk = jnp.where(incl, ak[:, CHUNK:], 0.0)
    av = _bmm(jnp.concatenate([a_ak, a_rk], axis=1), vh)
    akv, arkv = av[:, :CHUNK], av[:, CHUNK:]

    t_inv = eye + l_mat
    l_pow = l_mat
    for _ in range(int(math.log2(CHUNK)) - 1):
        l_pow = _bmm(l_pow, l_pow)
        t_inv = t_inv + _bmm(t_inv, l_pow)
    wt = _bmm(t_inv, at)
    u_loc = _bmm(t_inv, akv)

    st = st_ref[d]
    u = _bmm_nt(wt, st) + u_loc
    y = _bmm_nt(rt, st) + _bmm(a_rb, u) + arkv
    st_ref[d] = st * p_tot + _bmm_tn(u, bh) + _bmm_tn(vh, kh)
    y_ref[0, rows, :] = jnp.concatenate([y[h] for h in range(RW_HEADS)], axis=1)


def _rwkv_kernel(zf_ref, zfp_ref, zfn_ref, zb_ref, zbp_ref, zbn_ref, mu_ref, pv_ref, wup_ref, aup_ref,
                 gup_ref, seg_ref, tri_ref,
                 y0_ref, y1_ref, bon0_ref, bon1_ref, g_ref, ops_ref, st_ref):
    i = pl.program_id(1)
    n_tiles = pl.num_programs(1)
    tm = zf_ref.shape[1]

    @pl.when(i == 0)
    def _():
        st_ref[...] = jnp.zeros_like(st_ref)

    seg = seg_ref[...]
    for d, (z_ref, zp_ref, zn_ref, tile, bon_ref) in enumerate(
            ((zf_ref, zfp_ref, zfn_ref, i, bon0_ref), (zb_ref, zbp_ref, zbn_ref, n_tiles - 1 - i, bon1_ref))):
        p = _rwkv_prep(d, z_ref, zp_ref, zn_ref, tile, n_tiles, mu_ref, pv_ref, wup_ref, aup_ref, gup_ref, seg)
        for n, name in enumerate(("r", "v", "kd", "lw", "a", "b")):
            ops_ref[d, n] = p[name]
        bon_ref[0] = p["bonus"]
        if d == 0:
            g_ref[0] = p["g"]

    n_chunks = tm // CHUNK

    def body(c, carry):
        _rwkv_chunk(0, c, ops_ref, tri_ref, st_ref, y0_ref)
        _rwkv_chunk(1, n_chunks - 1 - c, ops_ref, tri_ref, st_ref, y1_ref)
        return carry

    lax.fori_loop(0, n_chunks, body, 0)


def _rwkv_scan(z, mu, pvec, wup, aup, gup, seg, tri, tm):
    B, S, _ = z.shape
    nt = S // tm
    rows8 = S // 8
    const = lambda shape: pl.BlockSpec(shape, lambda b, i: (0,) * len(shape))

    def tile_specs(tile_of):
        return [pl.BlockSpec((1, tm, RW_COLS), lambda b, i: (b, tile_of(i), 0)),
                pl.BlockSpec((1, 8, RW_COLS),
                             lambda b, i: (b, jnp.maximum(tile_of(i) * (tm // 8) - 1, 0), 0)),
                pl.BlockSpec((1, 8, RW_COLS),
                             lambda b, i: (b, jnp.minimum((tile_of(i) + 1) * (tm // 8), rows8 - 1), 0))]

    fwd = lambda i: i
    bwd = lambda i: nt - 1 - i
    out_f = pl.BlockSpec((1, tm, RW_WIDTH), lambda b, i: (b, i, 0))
    out_b = pl.BlockSpec((1, tm, RW_WIDTH), lambda b, i: (b, nt - 1 - i, 0))
    tok = jax.ShapeDtypeStruct((B, S, RW_WIDTH), F32)
    return pl.pallas_call(
        _rwkv_kernel,
        grid=(B, nt),
        in_specs=tile_specs(fwd) + tile_specs(bwd) + [
            const((2, RW_COLS)), const((PV_ROWS, RW_WIDTH)),
            const((2, LORA_IN, RW_WIDTH)), const((2, LORA_IN, RW_WIDTH)), const((GATE_LORA, RW_WIDTH)),
            const((RW_WIDTH, RW_WIDTH)), const((2, CHUNK, CHUNK))],
        out_specs=[out_f, out_b, out_f, out_b, out_f],
        out_shape=[tok] * 5,
        scratch_shapes=[pltpu.VMEM((2, 6, tm, RW_WIDTH), F32),
                        pltpu.VMEM((2, RW_HEADS, HEAD_DIM, HEAD_DIM), F32)],
        compiler_params=pltpu.CompilerParams(
            dimension_semantics=("arbitrary", "arbitrary"), vmem_limit_bytes=VMEM_LIMIT),
        name="rwkv_scan",
    )(z, z, z, z, z, z, mu, pvec, wup, aup, gup, seg, tri)


def _out_ffn_kernel(x_ref, oda_ref, y0_ref, y1_ref, bon0_ref, bon1_ref, g_ref, pv_ref, seg_ref,
                    wout_ref, g2_ref, w1_ref, w2_ref, o_ref, *, ff_chunk):
    seg = seg_ref[...]
    y = y0_ref[0] + y1_ref[0]
    mean = _seg_sum(y, seg) * (1.0 / HEAD_DIM)
    yc = y - mean
    var = _seg_sum(yc * yc, seg) * (1.0 / HEAD_DIM)
    yn = yc * lax.rsqrt(var + LN_X_EPS) * pv_ref[PV_LNG:PV_LNG + 1] + pv_ref[PV_LNB:PV_LNB + 1]
    o_rw = (yn + bon0_ref[0] + bon1_ref[0]) * g_ref[0]
    x = (x_ref[0] + _dot(oda_ref[0].astype(BF16), wout_ref[0:DA_WIDTH, :])
         + _dot(o_rw.astype(BF16), wout_ref[DA_WIDTH:D_MODEL, :]))
    ms = jnp.mean(x * x, axis=-1, keepdims=True)
    h = (x * lax.rsqrt(ms + NORM_EPS) * g2_ref[...]).astype(BF16)
    ffn = None
    for c in range(D_FF // ff_chunk):
        u = jnp.maximum(_dot(h, w1_ref[:, c * ff_chunk:(c + 1) * ff_chunk]), 0.0)
        part = _dot((u * u).astype(BF16), w2_ref[c * ff_chunk:(c + 1) * ff_chunk, :])
        ffn = part if ffn is None else ffn + part
    o_ref[0] = x + ffn


def _out_ffn(x, o_da, y0, y1, bon0, bon1, g, pvec, seg, w_out, g2, w1, w2, tm):
    B, S, _ = x.shape
    const = lambda shape: pl.BlockSpec(shape, lambda b, i: (0,) * len(shape))
    tok = lambda width: pl.BlockSpec((1, tm, width), lambda b, i: (b, i, 0))
    return pl.pallas_call(
        functools.partial(_out_ffn_kernel, ff_chunk=1024),
        grid=(B, S // tm),
        in_specs=[tok(D_MODEL)] + [tok(RW_WIDTH)] * 6 + [
            const((PV_ROWS, RW_WIDTH)), const((RW_WIDTH, RW_WIDTH)), const((D_MODEL, D_MODEL)),
            const((1, D_MODEL)), const((D_MODEL, D_FF)), const((D_FF, D_MODEL))],
        out_specs=tok(D_MODEL),
        out_shape=jax.ShapeDtypeStruct((B, S, D_MODEL), F32),
        compiler_params=pltpu.CompilerParams(
            dimension_semantics=("parallel", "parallel"), vmem_limit_bytes=VMEM_LIMIT),
        name="out_ffn",
    )(x, o_da, y0, y1, bon0, bon1, g, pvec, seg, w_out, g2, w1, w2)


def _rope_tables(seq_len):
    inv_freq = 1.0 / (ROPE_THETA ** (jnp.arange(0, HEAD_DIM, 2, dtype=F32) / HEAD_DIM))
    ang = jnp.arange(seq_len, dtype=F32)[:, None] * inv_freq[None, :]
    cos, sin = jnp.cos(ang), jnp.sin(ang)
    reps = LANES // HEAD_DIM
    cos_t = jnp.concatenate([cos, cos] * reps, axis=-1)
    sin_t = jnp.concatenate([-sin, sin] * reps, axis=-1)
    return cos_t, sin_t


def _pick(n, target):
    t = min(n, target)
    assert n % t == 0, (n, t)
    return t


def kernel(x_prompt, x_sample, norm1_g, w_in, q_norm_g, k_norm_g, lam_q1, lam_k1, lam_q2, lam_k2, subln_g,
           mu_prev, mu_next, w0, w_up, a0, a_up, g_up, k_k, k_a, r_k, ln_x_g, ln_x_b, w_out, norm2_g,
           w_ff1, w_ff2):
    l = 0
    w_in_b = w_in[l].astype(BF16)
    w_out_b = w_out[l].astype(BF16)
    w1_b = w_ff1[l].astype(BF16)
    w2_b = w_ff2[l].astype(BF16)
    g1 = norm1_g[l][None, :]
    g2 = norm2_g[l][None, :]
    qg = jnp.tile(q_norm_g[l], DA_WIDTH // HEAD_DIM)[None, :]
    kg = jnp.tile(k_norm_g[l], DA_WIDTH // HEAD_DIM)[None, :]
    lam_p = jnp.stack([lam_q1[l], lam_k1[l], lam_q2[l], lam_k2[l]])
    sg = subln_g[l][None, :]
    mu = jnp.stack([mu_prev[l], mu_next[l]])
    rows = [w0[l, 0], w0[l, 1], a0[l, 0], a0[l, 1], k_k[l], k_a[l], r_k[l].reshape(-1), ln_x_g[l], ln_x_b[l]]
    pvec = jnp.zeros((PV_ROWS, RW_WIDTH), F32).at[:len(rows)].set(jnp.stack(rows))
    zpad = jnp.zeros((2, DECAY_LORA, RW_WIDTH), F32)
    wup = jnp.concatenate([w_up[l], zpad], axis=1).astype(BF16)
    aup = jnp.concatenate([zpad, a_up[l]], axis=1).astype(BF16)
    gup = g_up[l].astype(BF16)
    ch = jnp.arange(RW_WIDTH) // HEAD_DIM
    seg = (ch[:, None] == ch[None, :]).astype(BF16)
    t = jnp.arange(CHUNK)
    tri = jnp.stack([t[None, :] <= t[:, None], t[None, :] >= t[:, None]]).astype(BF16)

    def run(x):
        S = x.shape[1]
        cos_t, sin_t = _rope_tables(S)
        q, k, v, z_rw = _in_proj(x, g1, w_in_b, qg, kg, cos_t, sin_t, seg, _pick(S, 512))
        o_da = _diff_attn(q, k, v, lam_p, sg, _pick(S, 256), _pick(S, 512))
        y0, y1, bon0, bon1, g = _rwkv_scan(z_rw, mu, pvec, wup, aup, gup, seg, tri, _pick(S, 256))
        return _out_ffn(x, o_da, y0, y1, bon0, bon1, g, pvec, seg, w_out_b, g2, w1_b, w2_b, _pick(S, 512))

    return (run(x_prompt), run(x_sample))
```

```python
import functools
import math

import jax
import jax.numpy as jnp
from jax import lax
from jax.experimental import pallas as pl
from jax.experimental.pallas import tpu as pltpu

F32 = jnp.float32
BF16 = jnp.bfloat16

D_MODEL = 1024
DA_HEADS = 4
HEAD_DIM = 64
DA_V_DIM = 128
DA_WIDTH = DA_HEADS * DA_V_DIM
RW_WIDTH = D_MODEL - DA_WIDTH
RW_HEADS = RW_WIDTH // HEAD_DIM
DECAY_LORA = 64
ICLR_LORA = 64
GATE_LORA = 128
LORA_IN = DECAY_LORA + ICLR_LORA
RW_COLS = 3 * RW_WIDTH + LORA_IN + GATE_LORA
DA_COLS = 3 * DA_WIDTH
IN_COLS = DA_COLS + RW_COLS
D_FF = 4 * D_MODEL
ROPE_THETA = 10000.0
NORM_EPS = 1e-6
LN_X_EPS = 64e-5
KK_EPS = 1e-12
LAMBDA_INIT = 0.8 - 0.6 * math.exp(-0.3 * 0)
LOG2E = 1.4426950408889634
QK_SCALE = HEAD_DIM ** -0.5

LANES = 128
CHUNK = 64
VMEM_LIMIT = 56 * 1024 * 1024

PV_W0, PV_A0, PV_KK, PV_KA, PV_RK, PV_LNG, PV_LNB = 0, 2, 4, 5, 6, 7, 8
PV_ROWS = 16


def _dot(a, b):
    return jnp.dot(a, b, preferred_element_type=F32)


def _split2(x):
    hi = x.astype(BF16)
    lo = (x - hi.astype(F32)).astype(BF16)
    return hi, lo


def _split3(x):
    hi = x.astype(BF16)
    r1 = x - hi.astype(F32)
    mid = r1.astype(BF16)
    lo = (r1 - mid.astype(F32)).astype(BF16)
    return hi, mid, lo


def _seg_sum(x, seg):
    hi, lo = _split2(x)
    return _dot(hi, seg) + _dot(lo, seg)


def _in_proj_kernel(x_ref, g1_ref, w_ref, qg_ref, kg_ref, cos_ref, sin_ref, seg_ref,
                    qt_ref, k_ref, vt_ref, z_ref):
    x = x_ref[0]
    ms = jnp.mean(x * x, axis=-1, keepdims=True)
    h = (x * lax.rsqrt(ms + NORM_EPS) * g1_ref[...]).astype(BF16)
    seg = seg_ref[...]
    reps = DA_WIDTH // LANES
    cos = jnp.concatenate([cos_ref[...]] * reps, axis=1)
    sin = jnp.concatenate([sin_ref[...]] * reps, axis=1)
    lane = lax.broadcasted_iota(jnp.int32, (1, LANES), 1)
    first_half = (lane % HEAD_DIM) < (HEAD_DIM // 2)

    def head_norm_rope(z, g):
        ss = _seg_sum(z * z, seg) * (1.0 / HEAD_DIM)
        zn = z * lax.rsqrt(ss + NORM_EPS) * g
        parts = []
        for c in range(reps):
            zc = zn[:, c * LANES:(c + 1) * LANES]
            parts.append(jnp.where(first_half,
                                   pltpu.roll(zc, LANES - HEAD_DIM // 2, 1),
                                   pltpu.roll(zc, HEAD_DIM // 2, 1)))
        rot = jnp.concatenate(parts, axis=1)
        return zn * cos + rot * sin

    zq = _dot(h, w_ref[:, 0:DA_WIDTH])
    qt_ref[0] = (head_norm_rope(zq, qg_ref[...]) * (QK_SCALE * LOG2E)).T.astype(BF16)
    zk = _dot(h, w_ref[:, DA_WIDTH:2 * DA_WIDTH])
    k_ref[0] = head_norm_rope(zk, kg_ref[...]).astype(BF16)
    vt_ref[0] = _dot(h, w_ref[:, 2 * DA_WIDTH:DA_COLS]).T.astype(BF16)
    z_ref[0] = _dot(h, w_ref[:, DA_COLS:IN_COLS])


def _in_proj(x, g1, w_in, qg, kg, cos_t, sin_t, seg, tm):
    B, S, _ = x.shape
    const = lambda shape: pl.BlockSpec(shape, lambda b, i: (0,) * len(shape))
    tok = lambda width: pl.BlockSpec((1, tm, width), lambda b, i: (b, i, 0))
    tok_t = pl.BlockSpec((1, DA_WIDTH, tm), lambda b, i: (b, 0, i))
    return pl.pallas_call(
        _in_proj_kernel,
        grid=(B, S // tm),
        in_specs=[tok(D_MODEL), const((1, D_MODEL)), const((D_MODEL, IN_COLS)),
                  const((1, DA_WIDTH)), const((1, DA_WIDTH)),
                  pl.BlockSpec((tm, LANES), lambda b, i: (i, 0)),
                  pl.BlockSpec((tm, LANES), lambda b, i: (i, 0)),
                  const((DA_WIDTH, DA_WIDTH))],
        out_specs=[tok_t, tok(DA_WIDTH), tok_t, tok(RW_COLS)],
        out_shape=[jax.ShapeDtypeStruct((B, DA_WIDTH, S), BF16), jax.ShapeDtypeStruct((B, S, DA_WIDTH), BF16),
                   jax.ShapeDtypeStruct((B, DA_WIDTH, S), BF16), jax.ShapeDtypeStruct((B, S, RW_COLS), F32)],
        compiler_params=pltpu.CompilerParams(
            dimension_semantics=("parallel", "parallel"), vmem_limit_bytes=VMEM_LIMIT),
        name="in_proj",
    )(x, g1, w_in, qg, kg, cos_t, sin_t, seg)


def _diff_attn_kernel(qt_ref, k_ref, vt_ref, lam_ref, sg_ref, o_ref, s_ref, acc_ref, *, tk):
    qt = qt_ref[0]
    tq = qt.shape[1]
    nk = k_ref.shape[1] // tk
    row = lax.broadcasted_iota(jnp.int32, (LANES, 1), 0)
    zero = jnp.zeros_like(qt)
    qts = (jnp.where(row < HEAD_DIM, qt, zero), jnp.where(row >= HEAD_DIM, qt, zero))
    acc_ref[...] = jnp.zeros_like(acc_ref)

    def scores(slot, blk):
        kb = k_ref[0, pl.ds(pl.multiple_of(blk * tk, tk), tk), :]
        mblk = []
        for c in range(2):
            s = _dot(kb, qts[c])
            s_ref[slot, c] = s
            mblk.append(jnp.max(s, axis=0, keepdims=True))
        return tuple(mblk)

    def consume(slot, blk, mblk, ml):
        vtb = vt_ref[0, :, pl.ds(pl.multiple_of(blk * tk, tk), tk)]
        out = []
        for c in range(2):
            m, l = ml[c]
            m_new = jnp.maximum(m, mblk[c])
            alpha = jnp.exp2(m - m_new)
            p = jnp.exp2(s_ref[slot, c] - m_new)
            l = alpha * l + jnp.sum(p, axis=0, keepdims=True)
            acc_ref[c] = alpha * acc_ref[c] + _dot(vtb, p.astype(BF16))
            out.append((m_new, l))
        return tuple(out)

    def body(i, carry):
        mblk, ml = carry
        mblk1 = scores(1, 2 * i + 1)
        ml = consume(0, 2 * i, mblk, ml)
        mblk0 = scores(0, 2 * i + 2)
        ml = consume(1, 2 * i + 1, mblk1, ml)
        return mblk0, ml

    ml = tuple((jnp.full((1, tq), -1e30, F32), jnp.zeros((1, tq), F32)) for _ in range(2))
    mblk, ml = lax.fori_loop(0, nk // 2 - 1, body, (scores(0, 0), ml))
    mblk1 = scores(1, nk - 1)
    ml = consume(0, nk - 2, mblk, ml)
    (_, l0), (_, l1) = consume(1, nk - 1, mblk1, ml)

    lp = lam_ref[...]
    lam = (jnp.exp(jnp.sum(lp[0:1] * lp[1:2], axis=1, keepdims=True))
           - jnp.exp(jnp.sum(lp[2:3] * lp[3:4], axis=1, keepdims=True)) + LAMBDA_INIT)
    o = (acc_ref[0] / l0 - lam * (acc_ref[1] / l1)).T
    ms = jnp.mean(o * o, axis=-1, keepdims=True)
    o_ref[0] = o * lax.rsqrt(ms + NORM_EPS) * sg_ref[...] * (1.0 - LAMBDA_INIT)


def _diff_attn(qt, k, vt, lam_p, subln_g, tq, tk):
    B, S, _ = k.shape
    assert (S // tk) % 2 == 0
    return pl.pallas_call(
        functools.partial(_diff_attn_kernel, tk=tk),
        grid=(B, DA_HEADS, S // tq),
        in_specs=[pl.BlockSpec((1, LANES, tq), lambda b, h, i: (b, h, i)),
                  pl.BlockSpec((1, S, LANES), lambda b, h, i: (b, 0, h)),
                  pl.BlockSpec((1, LANES, S), lambda b, h, i: (b, h, 0)),
                  pl.BlockSpec((4, HEAD_DIM), lambda b, h, i: (0, 0)),
                  pl.BlockSpec((1, DA_V_DIM), lambda b, h, i: (0, 0))],
        out_specs=pl.BlockSpec((1, tq, DA_V_DIM), lambda b, h, i: (b, i, h)),
        out_shape=jax.ShapeDtypeStruct((B, S, DA_WIDTH), F32),
        scratch_shapes=[pltpu.VMEM((2, 2, tk, tq), F32), pltpu.VMEM((2, DA_V_DIM, tq), F32)],
        compiler_params=pltpu.CompilerParams(
            dimension_semantics=("parallel", "parallel", "parallel"), vmem_limit_bytes=VMEM_LIMIT),
        name="diff_attn",
    )(qt, k, vt, lam_p, subln_g)


def _heads(x):
    return jnp.stack([x[:, h * HEAD_DIM:(h + 1) * HEAD_DIM] for h in range(RW_HEADS)])


def _bmm(a, b):
    return jnp.einsum("hts,hsu->htu", a.astype(BF16), b.astype(BF16), preferred_element_type=F32)


def _bmm_nt(a, b):
    return jnp.einsum("htj,hsj->hts", a.astype(BF16), b.astype(BF16), preferred_element_type=F32)


def _bmm_tn(a, b):
    return jnp.einsum("hsi,hsj->hij", a.astype(BF16), b.astype(BF16), preferred_element_type=F32)


def _rwkv_prep(d, z_ref, zp_ref, zn_ref, tile, n_tiles, mu_ref, pv_ref, wup_ref, aup_ref, gup_ref, seg):
    z = z_ref[0]
    tm = z.shape[0]
    prev_row = jnp.where(tile > 0, zp_ref[0, 7:8, :], 0.0)
    next_row = jnp.where(tile < n_tiles - 1, zn_ref[0, 0:1, :], 0.0)
    row = lax.broadcasted_iota(jnp.int32, (tm, 1), 0)
    z_prev = jnp.where(row == 0, prev_row, pltpu.roll(z, 1, 0))
    z_next = jnp.where(row == tm - 1, next_row, pltpu.roll(z, tm - 1, 0))
    zs = z + mu_ref[0:1] * (z_prev - z) + mu_ref[1:2] * (z_next - z)

    w = RW_WIDTH
    r, k, v = zs[:, 0:w], zs[:, w:2 * w], zs[:, 2 * w:3 * w]
    wa = zs[:, 3 * w:3 * w + LORA_IN]
    pv = lambda i: pv_ref[i:i + 1]
    w_pre = _dot(jnp.tanh(wa).astype(BF16), wup_ref[d])
    a_pre = _dot(wa.astype(BF16), aup_ref[d])
    u = -(pv(PV_W0 + d) + w_pre)
    softplus = jnp.maximum(u, 0.0) + jnp.log1p(jnp.exp(-jnp.abs(u)))
    w_log = -softplus - 0.5
    lw = -jnp.exp(w_log)
    a_rate = jax.nn.sigmoid(pv(PV_A0 + d) + a_pre)
    kk = k * pv(PV_KK)
    kk = kk * lax.rsqrt(_seg_sum(kk * kk, seg) + KK_EPS)
    kd = k * (1.0 + (a_rate - 1.0) * pv(PV_KA))
    bonus = _seg_sum(r * kd * pv(PV_RK), seg) * v
    out = dict(r=r, v=v, kd=kd, lw=lw, a=-kk, b=kk * a_rate, bonus=bonus)
    if d == 0:
        g_dn = zs[:, 3 * w + LORA_IN:RW_COLS]
        out["g"] = _dot(jax.nn.sigmoid(g_dn).astype(BF16), gup_ref[...])
    return out


def _rwkv_chunk(d, ci, ops_ref, tri_ref, st_ref, y_ref):
    rows = pl.ds(pl.multiple_of(ci * CHUNK, CHUNK), CHUNK)
    r, v, kd, lw, a, b = (ops_ref[d, n, rows, :] for n in range(6))
    tri = tri_ref[d]
    cl = sum(_dot(tri, part) for part in _split3(lw))
    tot = cl[CHUNK - 1:CHUNK] if d == 0 else cl[0:1]
    e_inv = jnp.exp(-cl)
    e_rem = jnp.exp(tot - cl)
    at = _heads(a * jnp.exp(cl - lw))
    rt = _heads(r * jnp.exp(cl))
    bt = _heads(b * e_inv)
    kt = _heads(kd * e_inv)
    bh = _heads(b * e_rem)
    kh = _heads(kd * e_rem)
    vh = _heads(v)
    p_tot = _heads(jnp.exp(tot))

    ti = lax.broadcasted_iota(jnp.int32, (CHUNK, CHUNK), 0)
    si = lax.broadcasted_iota(jnp.int32, (CHUNK, CHUNK), 1)
    strict = (si < ti) if d == 0 else (si > ti)
    incl = (si <= ti) if d == 0 else (si >= ti)
    eye = (si == ti).astype(F32)

    ar = jnp.concatenate([at, rt], axis=1)
    ab = _bmm_nt(ar, bt)
    ak = _bmm_nt(ar, kt)
    l_mat = jnp.where(strict, ab[:, :CHUNK], 0.0)
    a_ak = jnp.where(strict, ak[:, :CHUNK], 0.0)
    a_rb = jnp.where(incl, ab[:, CHUNK:], 0.0)
    a_rk = jnp.where(incl, ak[:, CHUNK:], 0.0)
    av = _bmm(jnp.concatenate([a_ak, a_rk], axis=1), vh)
    akv, arkv = av[:, :CHUNK], av[:, CHUNK:]

    t_inv = eye + l_mat
    l_pow = l_mat
    for _ in range(int(math.log2(CHUNK)) - 1):
        l_pow = _bmm(l_pow, l_pow)
        t_inv = t_inv + _bmm(t_inv, l_pow)
    wt = _bmm(t_inv, at)
    u_loc = _bmm(t_inv, akv)

    st = st_ref[d]
    u = _bmm_nt(wt, st) + u_loc
    y = _bmm_nt(rt, st) + _bmm(a_rb, u) + arkv
    st_ref[d] = st * p_tot + _bmm_tn(u, bh) + _bmm_tn(vh, kh)
    y_ref[0, rows, :] = jnp.concatenate([y[h] for h in range(RW_HEADS)], axis=1)


def _rwkv_kernel(zf_ref, zfp_ref, zfn_ref, zb_ref, zbp_ref, zbn_ref, mu_ref, pv_ref, wup_ref, aup_ref,
                 gup_ref, seg_ref, tri_ref,
                 y0_ref, y1_ref, bon0_ref, bon1_ref, g_ref, ops_ref, st_ref):
    i = pl.program_id(1)
    n_tiles = pl.num_programs(1)
    tm = zf_ref.shape[1]

    @pl.when(i == 0)
    def _():
        st_ref[...] = jnp.zeros_like(st_ref)

    seg = seg_ref[...]
    for d, (z_ref, zp_ref, zn_ref, tile, bon_ref) in enumerate(
            ((zf_ref, zfp_ref, zfn_ref, i, bon0_ref), (zb_ref, zbp_ref, zbn_ref, n_tiles - 1 - i, bon1_ref))):
        p = _rwkv_prep(d, z_ref, zp_ref, zn_ref, tile, n_tiles, mu_ref, pv_ref, wup_ref, aup_ref, gup_ref, seg)
        for n, name in enumerate(("r", "v", "kd", "lw", "a", "b")):
            ops_ref[d, n] = p[name]
        bon_ref[0] = p["bonus"]
        if d == 0:
            g_ref[0] = p["g"]

    n_chunks = tm // CHUNK

    def body(c, carry):
        _rwkv_chunk(0, c, ops_ref, tri_ref, st_ref, y0_ref)
        _rwkv_chunk(1, n_chunks - 1 - c, ops_ref, tri_ref, st_ref, y1_ref)
        return carry

    lax.fori_loop(0, n_chunks, body, 0)


def _rwkv_scan(z, mu, pvec, wup, aup, gup, seg, tri, tm):
    B, S, _ = z.shape
    nt = S // tm
    rows8 = S // 8
    const = lambda shape: pl.BlockSpec(shape, lambda b, i: (0,) * len(shape))

    def tile_specs(tile_of):
        return [pl.BlockSpec((1, tm, RW_COLS), lambda b, i: (b, tile_of(i), 0)),
                pl.BlockSpec((1, 8, RW_COLS),
                             lambda b, i: (b, jnp.maximum(tile_of(i) * (tm // 8) - 1, 0), 0)),
                pl.BlockSpec((1, 8, RW_COLS),
                             lambda b, i: (b, jnp.minimum((tile_of(i) + 1) * (tm // 8), rows8 - 1), 0))]

    fwd = lambda i: i
    bwd = lambda i: nt - 1 - i
    out_f = pl.BlockSpec((1, tm, RW_WIDTH), lambda b, i: (b, i, 0))
    out_b = pl.BlockSpec((1, tm, RW_WIDTH), lambda b, i: (b, nt - 1 - i, 0))
    tok = jax.ShapeDtypeStruct((B, S, RW_WIDTH), F32)
    return pl.pallas_call(
        _rwkv_kernel,
        grid=(B, nt),
        in_specs=tile_specs(fwd) + tile_specs(bwd) + [
            const((2, RW_COLS)), const((PV_ROWS, RW_WIDTH)),
            const((2, LORA_IN, RW_WIDTH)), const((2, LORA_IN, RW_WIDTH)), const((GATE_LORA, RW_WIDTH)),
            const((RW_WIDTH, RW_WIDTH)), const((2, CHUNK, CHUNK))],
        out_specs=[out_f, out_b, out_f, out_b, out_f],
        out_shape=[tok] * 5,
        scratch_shapes=[pltpu.VMEM((2, 6, tm, RW_WIDTH), F32),
                        pltpu.VMEM((2, RW_HEADS, HEAD_DIM, HEAD_DIM), F32)],
        compiler_params=pltpu.CompilerParams(
            dimension_semantics=("arbitrary", "arbitrary"), vmem_limit_bytes=VMEM_LIMIT),
        name="rwkv_scan",
    )(z, z, z, z, z, z, mu, pvec, wup, aup, gup, seg, tri)


def _out_ffn_kernel(x_ref, oda_ref, y0_ref, y1_ref, bon0_ref, bon1_ref, g_ref, pv_ref, seg_ref,
                    wout_ref, g2_ref, w1_ref, w2_ref, o_ref, *, ff_chunk):
    seg = seg_ref[...]
    y = y0_ref[0] + y1_ref[0]
    mean = _seg_sum(y, seg) * (1.0 / HEAD_DIM)
    yc = y - mean
    var = _seg_sum(yc * yc, seg) * (1.0 / HEAD_DIM)
    yn = yc * lax.rsqrt(var + LN_X_EPS) * pv_ref[PV_LNG:PV_LNG + 1] + pv_ref[PV_LNB:PV_LNB + 1]
    o_rw = (yn + bon0_ref[0] + bon1_ref[0]) * g_ref[0]
    x = (x_ref[0] + _dot(oda_ref[0].astype(BF16), wout_ref[0:DA_WIDTH, :])
         + _dot(o_rw.astype(BF16), wout_ref[DA_WIDTH:D_MODEL, :]))
    ms = jnp.mean(x * x, axis=-1, keepdims=True)
    h = (x * lax.rsqrt(ms + NORM_EPS) * g2_ref[...]).astype(BF16)
    ffn = None
    for c in range(D_FF // ff_chunk):
        u = jnp.maximum(_dot(h, w1_ref[:, c * ff_chunk:(c + 1) * ff_chunk]), 0.0)
        part = _dot((u * u).astype(BF16), w2_ref[c * ff_chunk:(c + 1) * ff_chunk, :])
        ffn = part if ffn is None else ffn + part
    o_ref[0] = x + ffn


def _out_ffn(x, o_da, y0, y1, bon0, bon1, g, pvec, seg, w_out, g2, w1, w2, tm):
    B, S, _ = x.shape
    const = lambda shape: pl.BlockSpec(shape, lambda b, i: (0,) * len(shape))
    tok = lambda width: pl.BlockSpec((1, tm, width), lambda b, i: (b, i, 0))
    return pl.pallas_call(
        functools.partial(_out_ffn_kernel, ff_chunk=1024),
        grid=(B, S // tm),
        in_specs=[tok(D_MODEL)] + [tok(RW_WIDTH)] * 6 + [
            const((PV_ROWS, RW_WIDTH)), const((RW_WIDTH, RW_WIDTH)), const((D_MODEL, D_MODEL)),
            const((1, D_MODEL)), const((D_MODEL, D_FF)), const((D_FF, D_MODEL))],
        out_specs=tok(D_MODEL),
        out_shape=jax.ShapeDtypeStruct((B, S, D_MODEL), F32),
        compiler_params=pltpu.CompilerParams(
            dimension_semantics=("parallel", "parallel"), vmem_limit_bytes=VMEM_LIMIT),
        name="out_ffn",
    )(x, o_da, y0, y1, bon0, bon1, g, pvec, seg, w_out, g2, w1, w2)


def _rope_tables(seq_len):
    inv_freq = 1.0 / (ROPE_THETA ** (jnp.arange(0, HEAD_DIM, 2, dtype=F32) / HEAD_DIM))
    ang = jnp.arange(seq_len, dtype=F32)[:, None] * inv_freq[None, :]
    cos, sin = jnp.cos(ang), jnp.sin(ang)
    reps = LANES // HEAD_DIM
    cos_t = jnp.concatenate([cos, cos] * reps, axis=-1)
    sin_t = jnp.concatenate([-sin, sin] * reps, axis=-1)
    return cos_t, sin_t


def _pick(n, target):
    t = min(n, target)
    assert n % t == 0, (n, t)
    return t


def kernel(x_prompt, x_sample, norm1_g, w_in, q_norm_g, k_norm_g, lam_q1, lam_k1, lam_q2, lam_k2, subln_g,
           mu_prev, mu_next, w0, w_up, a0, a_up, g_up, k_k, k_a, r_k, ln_x_g, ln_x_b, w_out, norm2_g,
           w_ff1, w_ff2):
    l = 0
    w_in_b = w_in[l].astype(BF16)
    w_out_b = w_out[l].astype(BF16)
    w1_b = w_ff1[l].astype(BF16)
    w2_b = w_ff2[l].astype(BF16)
    g1 = norm1_g[l][None, :]
    g2 = norm2_g[l][None, :]
    qg = jnp.tile(q_norm_g[l], DA_WIDTH // HEAD_DIM)[None, :]
    kg = jnp.tile(k_norm_g[l], DA_WIDTH // HEAD_DIM)[None, :]
    lam_p = jnp.stack([lam_q1[l], lam_k1[l], lam_q2[l], lam_k2[l]])
    sg = subln_g[l][None, :]
    mu = jnp.stack([mu_prev[l], mu_next[l]])
    rows = [w0[l, 0], w0[l, 1], a0[l, 0], a0[l, 1], k_k[l], k_a[l], r_k[l].reshape(-1), ln_x_g[l], ln_x_b[l]]
    pvec = jnp.zeros((PV_ROWS, RW_WIDTH), F32).at[:len(rows)].set(jnp.stack(rows))
    zpad = jnp.zeros((2, DECAY_LORA, RW_WIDTH), F32)
    wup = jnp.concatenate([w_up[l], zpad], axis=1).astype(BF16)
    aup = jnp.concatenate([zpad, a_up[l]], axis=1).astype(BF16)
    gup = g_up[l].astype(BF16)
    ch = jnp.arange(RW_WIDTH) // HEAD_DIM
    seg = (ch[:, None] == ch[None, :]).astype(BF16)
    t = jnp.arange(CHUNK)
    tri = jnp.stack([t[None, :] <= t[:, None], t[None, :] >= t[:, None]]).astype(BF16)

    def run(x):
        S = x.shape[1]
        cos_t, sin_t = _rope_tables(S)
        qt, k, vt, z_rw = _in_proj(x, g1, w_in_b, qg, kg, cos_t, sin_t, seg, _pick(S, 512))
        o_da = _diff_attn(qt, k, vt, lam_p, sg, _pick(S, 256), _pick(S // 2, 512))
        y0, y1, bon0, bon1, g = _rwkv_scan(z_rw, mu, pvec, wup, aup, gup, seg, tri, _pick(S, 256))
        return _out_ffn(x, o_da, y0, y1, bon0, bon1, g, pvec, seg, w_out_b, g2, w1_b, w2_b, _pick(S, 512))

    return (run(x_prompt), run(x_sample))
```

```python
import functools
import math

import jax
import jax.numpy as jnp
from jax import lax
from jax.experimental import pallas as pl
from jax.experimental.pallas import tpu as pltpu

F32 = jnp.float32
BF16 = jnp.bfloat16

D_MODEL = 1024
DA_HEADS = 4
HEAD_DIM = 64
DA_V_DIM = 128
DA_WIDTH = DA_HEADS * DA_V_DIM
RW_WIDTH = D_MODEL - DA_WIDTH
RW_HEADS = RW_WIDTH // HEAD_DIM
DECAY_LORA = 64
ICLR_LORA = 64
GATE_LORA = 128
LORA_IN = DECAY_LORA + ICLR_LORA
RW_COLS = 3 * RW_WIDTH + LORA_IN + GATE_LORA
DA_COLS = 3 * DA_WIDTH
IN_COLS = DA_COLS + RW_COLS
D_FF = 4 * D_MODEL
ROPE_THETA = 10000.0
NORM_EPS = 1e-6
LN_X_EPS = 64e-5
KK_EPS = 1e-12
LAMBDA_INIT = 0.8 - 0.6 * math.exp(-0.3 * 0)
LOG2E = 1.4426950408889634
QK_SCALE = HEAD_DIM ** -0.5

LANES = 128
CHUNK = 64
GROUP_HEADS = 4
CHUNKS_PER_ITER = 2
VMEM_LIMIT = 56 * 1024 * 1024

PV_W0, PV_A0, PV_KK, PV_KA, PV_RK, PV_LNG, PV_LNB = 0, 2, 4, 5, 6, 7, 8
PV_ROWS = 16


def _dot(a, b):
    return jnp.dot(a, b, preferred_element_type=F32)


def _split2(x):
    hi = x.astype(BF16)
    lo = (x - hi.astype(F32)).astype(BF16)
    return hi, lo


def _split3(x):
    hi = x.astype(BF16)
    r1 = x - hi.astype(F32)
    mid = r1.astype(BF16)
    lo = (r1 - mid.astype(F32)).astype(BF16)
    return hi, mid, lo


def _seg_sum(x, seg):
    hi, lo = _split2(x)
    return _dot(hi, seg) + _dot(lo, seg)


def _in_proj_kernel(x_ref, g1_ref, w_ref, qg_ref, kg_ref, cos_ref, sin_ref, seg_ref,
                    qt_ref, k_ref, vt_ref, z_ref):
    x = x_ref[0]
    ms = jnp.mean(x * x, axis=-1, keepdims=True)
    h = (x * lax.rsqrt(ms + NORM_EPS) * g1_ref[...]).astype(BF16)
    seg = seg_ref[...]
    reps = DA_WIDTH // LANES
    cos = jnp.concatenate([cos_ref[...]] * reps, axis=1)
    sin = jnp.concatenate([sin_ref[...]] * reps, axis=1)
    lane = lax.broadcasted_iota(jnp.int32, (1, LANES), 1)
    first_half = (lane % HEAD_DIM) < (HEAD_DIM // 2)

    def head_norm_rope(z, g):
        ss = _seg_sum(z * z, seg) * (1.0 / HEAD_DIM)
        zn = z * lax.rsqrt(ss + NORM_EPS) * g
        parts = []
        for c in range(reps):
            zc = zn[:, c * LANES:(c + 1) * LANES]
            parts.append(jnp.where(first_half,
                                   pltpu.roll(zc, LANES - HEAD_DIM // 2, 1),
                                   pltpu.roll(zc, HEAD_DIM // 2, 1)))
        rot = jnp.concatenate(parts, axis=1)
        return zn * cos + rot * sin

    zq = _dot(h, w_ref[:, 0:DA_WIDTH])
    qt_ref[0] = (head_norm_rope(zq, qg_ref[...]) * (QK_SCALE * LOG2E)).T.astype(BF16)
    zk = _dot(h, w_ref[:, DA_WIDTH:2 * DA_WIDTH])
    k_ref[0] = head_norm_rope(zk, kg_ref[...]).astype(BF16)
    vt_ref[0] = _dot(h, w_ref[:, 2 * DA_WIDTH:DA_COLS]).T.astype(BF16)
    z_ref[0] = _dot(h, w_ref[:, DA_COLS:IN_COLS])


def _in_proj(x, g1, w_in, qg, kg, cos_t, sin_t, seg, tm):
    B, S, _ = x.shape
    const = lambda shape: pl.BlockSpec(shape, lambda b, i: (0,) * len(shape))
    tok = lambda width: pl.BlockSpec((1, tm, width), lambda b, i: (b, i, 0))
    tok_t = pl.BlockSpec((1, DA_WIDTH, tm), lambda b, i: (b, 0, i))
    return pl.pallas_call(
        _in_proj_kernel,
        grid=(B, S // tm),
        in_specs=[tok(D_MODEL), const((1, D_MODEL)), const((D_MODEL, IN_COLS)),
                  const((1, DA_WIDTH)), const((1, DA_WIDTH)),
                  pl.BlockSpec((tm, LANES), lambda b, i: (i, 0)),
                  pl.BlockSpec((tm, LANES), lambda b, i: (i, 0)),
                  const((DA_WIDTH, DA_WIDTH))],
        out_specs=[tok_t, tok(DA_WIDTH), tok_t, tok(RW_COLS)],
        out_shape=[jax.ShapeDtypeStruct((B, DA_WIDTH, S), BF16), jax.ShapeDtypeStruct((B, S, DA_WIDTH), BF16),
                   jax.ShapeDtypeStruct((B, DA_WIDTH, S), BF16), jax.ShapeDtypeStruct((B, S, RW_COLS), F32)],
        compiler_params=pltpu.CompilerParams(
            dimension_semantics=("parallel", "parallel"), vmem_limit_bytes=VMEM_LIMIT),
        name="in_proj",
    )(x, g1, w_in, qg, kg, cos_t, sin_t, seg)


def _diff_attn_kernel(qt_ref, k_ref, vt_ref, lam_ref, sg_ref, o_ref, s_ref, acc_ref, *, tk, per_iter):
    qt = qt_ref[0]
    tq = qt.shape[1]
    nk = k_ref.shape[1] // tk
    row = lax.broadcasted_iota(jnp.int32, (LANES, 1), 0)
    zero = jnp.zeros_like(qt)
    qts = (jnp.where(row < HEAD_DIM, qt, zero), jnp.where(row >= HEAD_DIM, qt, zero))
    acc_ref[...] = jnp.zeros_like(acc_ref)

    def scores(slot, blk):
        kb = k_ref[0, pl.ds(pl.multiple_of(blk * tk, tk), tk), :]
        mblk = []
        for c in range(2):
            s = _dot(kb, qts[c])
            s_ref[slot, c] = s
            mblk.append(jnp.max(s, axis=0, keepdims=True))
        return tuple(mblk)

    def consume(slot, blk, mblk, ml):
        vtb = vt_ref[0, :, pl.ds(pl.multiple_of(blk * tk, tk), tk)]
        out = []
        for c in range(2):
            m, l = ml[c]
            m_new = jnp.maximum(m, mblk[c])
            alpha = jnp.exp2(m - m_new)
            p = jnp.exp2(s_ref[slot, c] - m_new)
            l = alpha * l + jnp.sum(p, axis=0, keepdims=True)
            acc_ref[c] = alpha * acc_ref[c] + _dot(vtb, p.astype(BF16))
            out.append((m_new, l))
        return tuple(out)

    def steps(first, carry, last):
        mblk, ml = carry
        for u in range(per_iter):
            nxt = None if (last and u == per_iter - 1) else scores((u + 1) % 2, first + u + 1)
            ml = consume(u % 2, first + u, mblk, ml)
            mblk = nxt
        return mblk, ml

    ml = tuple((jnp.full((1, tq), -1e30, F32), jnp.zeros((1, tq), F32)) for _ in range(2))
    carry = lax.fori_loop(0, nk // per_iter - 1, lambda i, c: steps(i * per_iter, c, False),
                          (scores(0, 0), ml))
    _, ((_, l0), (_, l1)) = steps(nk - per_iter, carry, True)

    lp = lam_ref[...]
    lam = (jnp.exp(jnp.sum(lp[0:1] * lp[1:2], axis=1, keepdims=True))
           - jnp.exp(jnp.sum(lp[2:3] * lp[3:4], axis=1, keepdims=True)) + LAMBDA_INIT)
    o = (acc_ref[0] / l0 - lam * (acc_ref[1] / l1)).T
    ms = jnp.mean(o * o, axis=-1, keepdims=True)
    o_ref[0] = o * lax.rsqrt(ms + NORM_EPS) * sg_ref[...] * (1.0 - LAMBDA_INIT)


def _diff_attn(qt, k, vt, lam_p, subln_g, tq, tk, per_iter):
    B, S, _ = k.shape
    assert per_iter % 2 == 0 and (S // tk) % per_iter == 0
    return pl.pallas_call(
        functools.partial(_diff_attn_kernel, tk=tk, per_iter=per_iter),
        grid=(B, DA_HEADS, S // tq),
        in_specs=[pl.BlockSpec((1, LANES, tq), lambda b, h, i: (b, h, i)),
                  pl.BlockSpec((1, S, LANES), lambda b, h, i: (b, 0, h)),
                  pl.BlockSpec((1, LANES, S), lambda b, h, i: (b, h, 0)),
                  pl.BlockSpec((4, HEAD_DIM), lambda b, h, i: (0, 0)),
                  pl.BlockSpec((1, DA_V_DIM), lambda b, h, i: (0, 0))],
        out_specs=pl.BlockSpec((1, tq, DA_V_DIM), lambda b, h, i: (b, i, h)),
        out_shape=jax.ShapeDtypeStruct((B, S, DA_WIDTH), F32),
        scratch_shapes=[pltpu.VMEM((2, 2, tk, tq), F32), pltpu.VMEM((2, DA_V_DIM, tq), F32)],
        compiler_params=pltpu.CompilerParams(
            dimension_semantics=("parallel", "parallel", "parallel"), vmem_limit_bytes=VMEM_LIMIT),
        name="diff_attn",
    )(qt, k, vt, lam_p, subln_g)


def _rwkv_prep(d, z_ref, zp_ref, zn_ref, tile, n_tiles, mu_ref, pv_ref, wup_ref, aup_ref, gup_ref, seg):
    z = z_ref[0]
    tm = z.shape[0]
    prev_row = jnp.where(tile > 0, zp_ref[0, 7:8, :], 0.0)
    next_row = jnp.where(tile < n_tiles - 1, zn_ref[0, 0:1, :], 0.0)
    row = lax.broadcasted_iota(jnp.int32, (tm, 1), 0)
    z_prev = jnp.where(row == 0, prev_row, pltpu.roll(z, 1, 0))
    z_next = jnp.where(row == tm - 1, next_row, pltpu.roll(z, tm - 1, 0))
    zs = z + mu_ref[0:1] * (z_prev - z) + mu_ref[1:2] * (z_next - z)

    w = RW_WIDTH
    r, k, v = zs[:, 0:w], zs[:, w:2 * w], zs[:, 2 * w:3 * w]
    wa = zs[:, 3 * w:3 * w + LORA_IN]
    pv = lambda i: pv_ref[i:i + 1]
    w_pre = _dot(jnp.tanh(wa).astype(BF16), wup_ref[d])
    a_pre = _dot(wa.astype(BF16), aup_ref[d])
    lw = -math.exp(-0.5) * jax.nn.sigmoid(pv(PV_W0 + d) + w_pre)
    a_rate = jax.nn.sigmoid(pv(PV_A0 + d) + a_pre)
    kk = k * pv(PV_KK)
    kk = kk * lax.rsqrt(_dot((kk * kk).astype(BF16), seg) + KK_EPS)
    kd = k * (1.0 + (a_rate - 1.0) * pv(PV_KA))
    bonus = _dot((r * kd * pv(PV_RK)).astype(BF16), seg) * v
    out = dict(r=r, v=v, kd=kd, lw=lw, a=-kk, b=kk * a_rate, bonus=bonus)
    if d == 0:
        g_dn = zs[:, 3 * w + LORA_IN:RW_COLS]
        out["g"] = _dot(jax.nn.sigmoid(g_dn).astype(BF16), gup_ref[...])
    return out


def _rwkv_chunks(cis, ops_ref, tri_ref, seg_ref, st_ref, y_refs):
    gw = GROUP_HEADS * HEAD_DIM
    n_groups = RW_HEADS // GROUP_HEADS
    bmask = seg_ref[0:gw, 0:gw]
    ti = lax.broadcasted_iota(jnp.int32, (CHUNK, gw), 0)
    si = lax.broadcasted_iota(jnp.int32, (CHUNK, gw), 1) % HEAD_DIM
    eye = (si == ti).astype(F32)
    strict = ((si < ti), (si > ti))
    incl = ((si <= ti), (si >= ti))
    nt = (((1,), (1,)), ((), ()))
    tn = (((0,), (0,)), ((), ()))

    def blockdiag(x):
        return jnp.concatenate([x.astype(BF16)] * GROUP_HEADS, axis=0) * bmask

    def stack(*xs):
        return jnp.concatenate([x.astype(BF16) for x in xs], axis=0)

    n_seq = len(cis[0])
    rows = [[pl.ds(pl.multiple_of(ci * CHUNK, CHUNK), CHUNK) for ci in cis[d]] for d in range(2)]
    chains = []
    for k in range(n_seq):
        for d in range(2):
            r, v, kd, lw, a, b = (ops_ref[d, n, rows[d][k], :] for n in range(6))
            cl = sum(_dot(tri_ref[d], part) for part in _split3(lw))
            tot = cl[CHUNK - 1:CHUNK] if d == 0 else cl[0:1]
            e_inv = jnp.exp(-cl)
            e_rem = jnp.exp(tot - cl)
            wide = dict(at=a * jnp.exp(cl - lw), rt=r * jnp.exp(cl), bt=b * e_inv, kt=kd * e_inv,
                        bh=b * e_rem, kh=kd * e_rem, v=v, decay=jnp.exp(tot))
            for g in range(n_groups):
                c = {name: x[:, g * gw:(g + 1) * gw] for name, x in wide.items()}
                c.update(k=k, d=d, g=g)
                chains.append(c)

    for c in chains:
        ar = stack(c["at"], c["rt"])
        c["ab"] = lax.dot_general(ar, blockdiag(c["bt"]), nt, preferred_element_type=F32)
        c["ak"] = lax.dot_general(ar, blockdiag(c["kt"]), nt, preferred_element_type=F32)
    for c in chains:
        d = c["d"]
        c["l"] = jnp.where(strict[d], c["ab"][:CHUNK], 0.0)
        c["a_rb"] = jnp.where(incl[d], c["ab"][CHUNK:], 0.0)
        a_ak = jnp.where(strict[d], c["ak"][:CHUNK], 0.0)
        a_rk = jnp.where(incl[d], c["ak"][CHUNK:], 0.0)
        av = _dot(stack(a_ak, a_rk), blockdiag(c["v"]))
        c["akv"], c["arkv"] = av[:CHUNK], av[CHUNK:]

    for c in chains:
        c["t"] = eye + c["l"]
        c["lp"] = _dot(c["l"].astype(BF16), blockdiag(c["l"]))
    for _ in range(int(math.log2(CHUNK)) - 2):
        for c in chains:
            both = _dot(stack(c["t"], c["lp"]), blockdiag(c["lp"]))
            c["t"] = c["t"] + both[:CHUNK]
            c["lp"] = both[CHUNK:]
    for c in chains:
        c["t"] = (c["t"] + _dot(c["t"].astype(BF16), blockdiag(c["lp"]))).astype(BF16)
    for c in chains:
        c["wt"] = _dot(c["t"], blockdiag(c["at"]))
        c["u_loc"] = _dot(c["t"], blockdiag(c["akv"]))

    state = {(d, g): st_ref[d, g] for d in range(2) for g in range(n_groups)}
    for k in range(n_seq):
        now = [c for c in chains if c["k"] == k]
        for c in now:
            wr = _dot(stack(c["wt"], c["rt"]), blockdiag(state[c["d"], c["g"]]))
            c["u"] = wr[:CHUNK] + c["u_loc"]
            c["y"] = wr[CHUNK:] + c["arkv"]
        for c in now:
            c["y"] = c["y"] + _dot(c["a_rb"].astype(BF16), blockdiag(c["u"]))
            st_hi, st_lo = _split2(state[c["d"], c["g"]])
            decay = eye * c["decay"]
            full = lax.dot_general(stack(c["bh"], c["kh"], decay, decay), stack(c["u"], c["v"], st_hi, st_lo),
                                   tn, preferred_element_type=F32) * bmask
            state[c["d"], c["g"]] = sum(full[h * HEAD_DIM:(h + 1) * HEAD_DIM] for h in range(GROUP_HEADS))
        for d in range(2):
            y_refs[d][0, rows[d][k], :] = jnp.concatenate([c["y"] for c in now if c["d"] == d], axis=1)
    for (d, g), st in state.items():
        st_ref[d, g] = st


def _rwkv_kernel(zf_ref, zfp_ref, zfn_ref, zb_ref, zbp_ref, zbn_ref, mu_ref, pv_ref, wup_ref, aup_ref,
                 gup_ref, seg_ref, tri_ref,
                 y0_ref, y1_ref, bon0_ref, bon1_ref, g_ref, ops_ref, st_ref):
    i = pl.program_id(1)
    n_tiles = pl.num_programs(1)
    tm = zf_ref.shape[1]

    @pl.when(i == 0)
    def _():
        st_ref[...] = jnp.zeros_like(st_ref)

    seg = seg_ref[...]
    for d, (z_ref, zp_ref, zn_ref, tile, bon_ref) in enumerate(
            ((zf_ref, zfp_ref, zfn_ref, i, bon0_ref), (zb_ref, zbp_ref, zbn_ref, n_tiles - 1 - i, bon1_ref))):
        p = _rwkv_prep(d, z_ref, zp_ref, zn_ref, tile, n_tiles, mu_ref, pv_ref, wup_ref, aup_ref, gup_ref, seg)
        for n, name in enumerate(("r", "v", "kd", "lw", "a", "b")):
            ops_ref[d, n] = p[name]
        bon_ref[0] = p["bonus"]
        if d == 0:
            g_ref[0] = p["g"]

    n_chunks = tm // CHUNK
    per_iter = min(CHUNKS_PER_ITER, n_chunks)

    def body(it, carry):
        fwd = [it * per_iter + k for k in range(per_iter)]
        bwd = [n_chunks - 1 - c for c in fwd]
        _rwkv_chunks((fwd, bwd), ops_ref, tri_ref, seg_ref, st_ref, (y0_ref, y1_ref))
        return carry

    lax.fori_loop(0, n_chunks // per_iter, body, 0)


def _rwkv_scan(z, mu, pvec, wup, aup, gup, seg, tri, tm):
    B, S, _ = z.shape
    nt = S // tm
    rows8 = S // 8
    const = lambda shape: pl.BlockSpec(shape, lambda b, i: (0,) * len(shape))

    def tile_specs(tile_of):
        return [pl.BlockSpec((1, tm, RW_COLS), lambda b, i: (b, tile_of(i), 0)),
                pl.BlockSpec((1, 8, RW_COLS),
                             lambda b, i: (b, jnp.maximum(tile_of(i) * (tm // 8) - 1, 0), 0)),
                pl.BlockSpec((1, 8, RW_COLS),
                             lambda b, i: (b, jnp.minimum((tile_of(i) + 1) * (tm // 8), rows8 - 1), 0))]

    fwd = lambda i: i
    bwd = lambda i: nt - 1 - i
    out_f = pl.BlockSpec((1, tm, RW_WIDTH), lambda b, i: (b, i, 0))
    out_b = pl.BlockSpec((1, tm, RW_WIDTH), lambda b, i: (b, nt - 1 - i, 0))
    tok = jax.ShapeDtypeStruct((B, S, RW_WIDTH), F32)
    return pl.pallas_call(
        _rwkv_kernel,
        grid=(B, nt),
        in_specs=tile_specs(fwd) + tile_specs(bwd) + [
            const((2, RW_COLS)), const((PV_ROWS, RW_WIDTH)),
            const((2, LORA_IN, RW_WIDTH)), const((2, LORA_IN, RW_WIDTH)), const((GATE_LORA, RW_WIDTH)),
            const((RW_WIDTH, RW_WIDTH)), const((2, CHUNK, CHUNK))],
        out_specs=[out_f, out_b, out_f, out_b, out_f],
        out_shape=[tok] * 5,
        scratch_shapes=[pltpu.VMEM((2, 6, tm, RW_WIDTH), F32),
                        pltpu.VMEM((2, RW_HEADS // GROUP_HEADS, HEAD_DIM, GROUP_HEADS * HEAD_DIM), F32)],
        compiler_params=pltpu.CompilerParams(
            dimension_semantics=("arbitrary", "arbitrary"), vmem_limit_bytes=VMEM_LIMIT),
        name="rwkv_scan",
    )(z, z, z, z, z, z, mu, pvec, wup, aup, gup, seg, tri)


def _out_ffn_kernel(x_ref, oda_ref, y0_ref, y1_ref, bon0_ref, bon1_ref, g_ref, pv_ref, seg_ref,
                    wout_ref, g2_ref, w1_ref, w2_ref, o_ref, *, ff_chunk):
    seg = seg_ref[...]
    y = y0_ref[0] + y1_ref[0]
    mean = _seg_sum(y, seg) * (1.0 / HEAD_DIM)
    yc = y - mean
    var = _seg_sum(yc * yc, seg) * (1.0 / HEAD_DIM)
    yn = yc * lax.rsqrt(var + LN_X_EPS) * pv_ref[PV_LNG:PV_LNG + 1] + pv_ref[PV_LNB:PV_LNB + 1]
    o_rw = (yn + bon0_ref[0] + bon1_ref[0]) * g_ref[0]
    x = (x_ref[0] + _dot(oda_ref[0].astype(BF16), wout_ref[0:DA_WIDTH, :])
         + _dot(o_rw.astype(BF16), wout_ref[DA_WIDTH:D_MODEL, :]))
    ms = jnp.mean(x * x, axis=-1, keepdims=True)
    h = (x * lax.rsqrt(ms + NORM_EPS) * g2_ref[...]).astype(BF16)
    ffn = None
    for c in range(D_FF // ff_chunk):
        u = jnp.maximum(_dot(h, w1_ref[:, c * ff_chunk:(c + 1) * ff_chunk]), 0.0)
        part = _dot((u * u).astype(BF16), w2_ref[c * ff_chunk:(c + 1) * ff_chunk, :])
        ffn = part if ffn is None else ffn + part
    o_ref[0] = x + ffn


def _out_ffn(x, o_da, y0, y1, bon0, bon1, g, pvec, seg, w_out, g2, w1, w2, tm):
    B, S, _ = x.shape
    const = lambda shape: pl.BlockSpec(shape, lambda b, i: (0,) * len(shape))
    tok = lambda width: pl.BlockSpec((1, tm, width), lambda b, i: (b, i, 0))
    return pl.pallas_call(
        functools.partial(_out_ffn_kernel, ff_chunk=1024),
        grid=(B, S // tm),
        in_specs=[tok(D_MODEL)] + [tok(RW_WIDTH)] * 6 + [
            const((PV_ROWS, RW_WIDTH)), const((RW_WIDTH, RW_WIDTH)), const((D_MODEL, D_MODEL)),
            const((1, D_MODEL)), const((D_MODEL, D_FF)), const((D_FF, D_MODEL))],
        out_specs=tok(D_MODEL),
        out_shape=jax.ShapeDtypeStruct((B, S, D_MODEL), F32),
        compiler_params=pltpu.CompilerParams(
            dimension_semantics=("parallel", "parallel"), vmem_limit_bytes=VMEM_LIMIT),
        name="out_ffn",
    )(x, o_da, y0, y1, bon0, bon1, g, pvec, seg, w_out, g2, w1, w2)


def _rope_tables(seq_len):
    inv_freq = 1.0 / (ROPE_THETA ** (jnp.arange(0, HEAD_DIM, 2, dtype=F32) / HEAD_DIM))
    ang = jnp.arange(seq_len, dtype=F32)[:, None] * inv_freq[None, :]
    cos, sin = jnp.cos(ang), jnp.sin(ang)
    reps = LANES // HEAD_DIM
    cos_t = jnp.concatenate([cos, cos] * reps, axis=-1)
    sin_t = jnp.concatenate([-sin, sin] * reps, axis=-1)
    return cos_t, sin_t


def _attn_tiles(seq_len):
    if seq_len >= 8192:
        return 256, 1024, 4
    tk = _pick(seq_len // 2, 256)
    return _pick(seq_len, 1024), tk, 2


def _pick(n, target):
    t = min(n, target)
    assert n % t == 0, (n, t)
    return t


def kernel(x_prompt, x_sample, norm1_g, w_in, q_norm_g, k_norm_g, lam_q1, lam_k1, lam_q2, lam_k2, subln_g,
           mu_prev, mu_next, w0, w_up, a0, a_up, g_up, k_k, k_a, r_k, ln_x_g, ln_x_b, w_out, norm2_g,
           w_ff1, w_ff2):
    l = 0
    w_in_b = w_in[l].astype(BF16)
    w_out_b = w_out[l].astype(BF16)
    w1_b = w_ff1[l].astype(BF16)
    w2_b = w_ff2[l].astype(BF16)
    g1 = norm1_g[l][None, :]
    g2 = norm2_g[l][None, :]
    qg = jnp.tile(q_norm_g[l], DA_WIDTH // HEAD_DIM)[None, :]
    kg = jnp.tile(k_norm_g[l], DA_WIDTH // HEAD_DIM)[None, :]
    lam_p = jnp.stack([lam_q1[l], lam_k1[l], lam_q2[l], lam_k2[l]])
    sg = subln_g[l][None, :]
    mu = jnp.stack([mu_prev[l], mu_next[l]])
    rows = [w0[l, 0], w0[l, 1], a0[l, 0], a0[l, 1], k_k[l], k_a[l], r_k[l].reshape(-1), ln_x_g[l], ln_x_b[l]]
    pvec = jnp.zeros((PV_ROWS, RW_WIDTH), F32).at[:len(rows)].set(jnp.stack(rows))
    zpad = jnp.zeros((2, DECAY_LORA, RW_WIDTH), F32)
    wup = jnp.concatenate([w_up[l], zpad], axis=1).astype(BF16)
    aup = jnp.concatenate([zpad, a_up[l]], axis=1).astype(BF16)
    gup = g_up[l].astype(BF16)
    ch = jnp.arange(RW_WIDTH) // HEAD_DIM
    seg = (ch[:, None] == ch[None, :]).astype(BF16)
    t = jnp.arange(CHUNK)
    tri = jnp.stack([t[None, :] <= t[:, None], t[None, :] >= t[:, None]]).astype(BF16)

    def run(x):
        S = x.shape[1]
        cos_t, sin_t = _rope_tables(S)
        qt, k, vt, z_rw = _in_proj(x, g1, w_in_b, qg, kg, cos_t, sin_t, seg, _pick(S, 512))
        tq, tk, per_iter = _attn_tiles(S)
        o_da = _diff_attn(qt, k, vt, lam_p, sg, tq, tk, per_iter)
        y0, y1, bon0, bon1, g = _rwkv_scan(z_rw, mu, pvec, wup, aup, gup, seg, tri, _pick(S, 256))
        return _out_ffn(x, o_da, y0, y1, bon0, bon1, g, pvec, seg, w_out_b, g2, w1_b, w2_b, _pick(S, 512))

    return (run(x_prompt), run(x_sample))
```

```python
import functools
import math

import jax
import jax.numpy as jnp
from jax import lax
from jax.experimental import pallas as pl
from jax.experimental.pallas import tpu as pltpu

F32 = jnp.float32
BF16 = jnp.bfloat16

D_MODEL = 1024
DA_HEADS = 4
HEAD_DIM = 64
DA_V_DIM = 128
DA_WIDTH = DA_HEADS * DA_V_DIM
RW_WIDTH = D_MODEL - DA_WIDTH
RW_HEADS = RW_WIDTH // HEAD_DIM
DECAY_LORA = 64
ICLR_LORA = 64
GATE_LORA = 128
LORA_IN = DECAY_LORA + ICLR_LORA
RW_COLS = 3 * RW_WIDTH + LORA_IN + GATE_LORA
DA_COLS = 3 * DA_WIDTH
IN_COLS = DA_COLS + RW_COLS
D_FF = 4 * D_MODEL
ROPE_THETA = 10000.0
NORM_EPS = 1e-6
LN_X_EPS = 64e-5
KK_EPS = 1e-12
LAMBDA_INIT = 0.8 - 0.6 * math.exp(-0.3 * 0)
LOG2E = 1.4426950408889634
QK_SCALE = HEAD_DIM ** -0.5

LANES = 128
CHUNK = 64
GROUP_HEADS = 4
SEG_WIDTH = GROUP_HEADS * HEAD_DIM
CHUNKS_PER_ITER = 2
VMEM_LIMIT = 56 * 1024 * 1024

PV_W0, PV_A0, PV_KK, PV_KA, PV_RK, PV_LNG, PV_LNB = 0, 2, 4, 5, 6, 7, 8
PV_ROWS = 16


def _dot(a, b):
    return jnp.dot(a, b, preferred_element_type=F32)


def _split2(x):
    hi = x.astype(BF16)
    lo = (x - hi.astype(F32)).astype(BF16)
    return hi, lo


def _split3(x):
    hi = x.astype(BF16)
    r1 = x - hi.astype(F32)
    mid = r1.astype(BF16)
    lo = (r1 - mid.astype(F32)).astype(BF16)
    return hi, mid, lo


def _seg_sum(x, seg):
    xb = x.astype(BF16)
    gw = seg.shape[0]
    return jnp.concatenate([_dot(xb[:, g * gw:(g + 1) * gw], seg) for g in range(x.shape[1] // gw)], axis=1)


def _in_proj_kernel(x_ref, xp_ref, xn_ref, g1_ref, w_ref, qg_ref, kg_ref, cos_ref, sin_ref, seg_ref, mu_ref,
                    qt_ref, k_ref, vt_ref, z_ref):
    def norm1(x):
        ms = jnp.mean(x * x, axis=-1, keepdims=True)
        return (x * lax.rsqrt(ms + NORM_EPS) * g1_ref[...]).astype(BF16)

    h = norm1(x_ref[0])
    tm = h.shape[0]
    seg = seg_ref[...]
    reps = DA_WIDTH // LANES
    cos = jnp.concatenate([cos_ref[...]] * reps, axis=1)
    sin = jnp.concatenate([sin_ref[...]] * reps, axis=1)
    lane = lax.broadcasted_iota(jnp.int32, (1, LANES), 1)
    first_half = (lane % HEAD_DIM) < (HEAD_DIM // 2)

    def head_norm_rope(z, g):
        ss = _seg_sum(z * z, seg) * (1.0 / HEAD_DIM)
        zn = z * lax.rsqrt(ss + NORM_EPS) * g
        parts = []
        for c in range(reps):
            zc = zn[:, c * LANES:(c + 1) * LANES]
            parts.append(jnp.where(first_half,
                                   pltpu.roll(zc, LANES - HEAD_DIM // 2, 1),
                                   pltpu.roll(zc, HEAD_DIM // 2, 1)))
        rot = jnp.concatenate(parts, axis=1)
        return zn * cos + rot * sin

    zq = _dot(h, w_ref[:, 0:DA_WIDTH])
    qt_ref[0] = (head_norm_rope(zq, qg_ref[...]) * (QK_SCALE * LOG2E)).T.astype(BF16)
    zk = _dot(h, w_ref[:, DA_WIDTH:2 * DA_WIDTH])
    k_ref[0] = head_norm_rope(zk, kg_ref[...]).astype(BF16)
    vt_ref[0] = _dot(h, w_ref[:, 2 * DA_WIDTH:DA_COLS]).T.astype(BF16)

    i = pl.program_id(1)
    h_ext = jnp.concatenate([h, norm1(xp_ref[0]), norm1(xn_ref[0])], axis=0)
    row = lax.broadcasted_iota(jnp.int32, (tm, 1), 0)
    slab = lambda c0: _dot(h_ext, w_ref[:, DA_COLS + c0:DA_COLS + c0 + SEG_WIDTH])
    z_next_group = slab(0)
    for c0 in range(0, RW_COLS, SEG_WIDTH):
        cols = slice(c0, c0 + SEG_WIDTH)
        z_ext = z_next_group
        if c0 + SEG_WIDTH < RW_COLS:
            z_next_group = slab(c0 + SEG_WIDTH)
        z = z_ext[:tm]
        prev_row = jnp.where(i > 0, z_ext[tm + 7:tm + 8], 0.0)
        next_row = jnp.where(i < pl.num_programs(1) - 1, z_ext[tm + 8:tm + 9], 0.0)
        z_prev = jnp.where(row == 0, prev_row, pltpu.roll(z, 1, 0))
        z_next = jnp.where(row == tm - 1, next_row, pltpu.roll(z, tm - 1, 0))
        z_ref[0, :, cols] = z + mu_ref[0:1, cols] * (z_prev - z) + mu_ref[1:2, cols] * (z_next - z)


def _in_proj(x, g1, w_in, qg, kg, cos_t, sin_t, seg, mu, tm):
    B, S, _ = x.shape
    const = lambda shape: pl.BlockSpec(shape, lambda b, i: (0,) * len(shape))
    tok = lambda width: pl.BlockSpec((1, tm, width), lambda b, i: (b, i, 0))
    tok_t = pl.BlockSpec((1, DA_WIDTH, tm), lambda b, i: (b, 0, i))
    halo_prev = pl.BlockSpec((1, 8, D_MODEL), lambda b, i: (b, jnp.maximum(i * (tm // 8) - 1, 0), 0))
    halo_next = pl.BlockSpec((1, 8, D_MODEL), lambda b, i: (b, jnp.minimum((i + 1) * (tm // 8), S // 8 - 1), 0))
    return pl.pallas_call(
        _in_proj_kernel,
        grid=(B, S // tm),
        in_specs=[tok(D_MODEL), halo_prev, halo_next, const((1, D_MODEL)), const((D_MODEL, IN_COLS)),
                  const((1, DA_WIDTH)), const((1, DA_WIDTH)),
                  pl.BlockSpec((tm, LANES), lambda b, i: (i, 0)),
                  pl.BlockSpec((tm, LANES), lambda b, i: (i, 0)),
                  const((SEG_WIDTH, SEG_WIDTH)), const((2, RW_COLS))],
        out_specs=[tok_t, tok(DA_WIDTH), tok_t, tok(RW_COLS)],
        out_shape=[jax.ShapeDtypeStruct((B, DA_WIDTH, S), BF16), jax.ShapeDtypeStruct((B, S, DA_WIDTH), BF16),
                   jax.ShapeDtypeStruct((B, DA_WIDTH, S), BF16), jax.ShapeDtypeStruct((B, S, RW_COLS), F32)],
        compiler_params=pltpu.CompilerParams(
            dimension_semantics=("parallel", "parallel"), vmem_limit_bytes=VMEM_LIMIT),
        name="in_proj",
    )(x, x, x, g1, w_in, qg, kg, cos_t, sin_t, seg, mu)


def _diff_attn_kernel(qt_ref, k_ref, vt_ref, lam_ref, sg_ref, o_ref, s_ref, acc_ref, *, tk, per_iter):
    qt = qt_ref[0]
    tq = qt.shape[1]
    nk = k_ref.shape[1] // tk
    row = lax.broadcasted_iota(jnp.int32, (LANES, 1), 0)
    zero = jnp.zeros_like(qt)
    qts = (jnp.where(row < HEAD_DIM, qt, zero), jnp.where(row >= HEAD_DIM, qt, zero))
    acc_ref[...] = jnp.zeros_like(acc_ref)

    def scores(slot, blk):
        kb = k_ref[0, pl.ds(pl.multiple_of(blk * tk, tk), tk), :]
        mblk = []
        for c in range(2):
            s = _dot(kb, qts[c])
            s_ref[slot, c] = s
            mblk.append(jnp.max(s, axis=0, keepdims=True))
        return tuple(mblk)

    def consume(slot, blk, mblk, ml):
        vtb = vt_ref[0, :, pl.ds(pl.multiple_of(blk * tk, tk), tk)]
        out = []
        for c in range(2):
            m, l = ml[c]
            m_new = jnp.maximum(m, mblk[c])
            alpha = jnp.exp2(m - m_new)
            p = jnp.exp2(s_ref[slot, c] - m_new)
            l = alpha * l + jnp.sum(p, axis=0, keepdims=True)
            acc_ref[c] = alpha * acc_ref[c] + _dot(vtb, p.astype(BF16))
            out.append((m_new, l))
        return tuple(out)

    def steps(first, carry, last):
        mblk, ml = carry
        for u in range(per_iter):
            nxt = None if (last and u == per_iter - 1) else scores((u + 1) % 2, first + u + 1)
            ml = consume(u % 2, first + u, mblk, ml)
            mblk = nxt
        return mblk, ml

    ml = tuple((jnp.full((1, tq), -1e30, F32), jnp.zeros((1, tq), F32)) for _ in range(2))
    carry = lax.fori_loop(0, nk // per_iter - 1, lambda i, c: steps(i * per_iter, c, False),
                          (scores(0, 0), ml))
    _, ((_, l0), (_, l1)) = steps(nk - per_iter, carry, True)

    lp = lam_ref[...]
    lam = (jnp.exp(jnp.sum(lp[0:1] * lp[1:2], axis=1, keepdims=True))
           - jnp.exp(jnp.sum(lp[2:3] * lp[3:4], axis=1, keepdims=True)) + LAMBDA_INIT)
    o = (acc_ref[0] / l0 - lam * (acc_ref[1] / l1)).T
    ms = jnp.mean(o * o, axis=-1, keepdims=True)
    o_ref[0] = o * lax.rsqrt(ms + NORM_EPS) * sg_ref[...] * (1.0 - LAMBDA_INIT)


def _diff_attn(qt, k, vt, lam_p, subln_g, tq, tk, per_iter):
    B, S, _ = k.shape
    assert per_iter % 2 == 0 and (S // tk) % per_iter == 0
    return pl.pallas_call(
        functools.partial(_diff_attn_kernel, tk=tk, per_iter=per_iter),
        grid=(B, DA_HEADS, S // tq),
        in_specs=[pl.BlockSpec((1, LANES, tq), lambda b, h, i: (b, h, i)),
                  pl.BlockSpec((1, S, LANES), lambda b, h, i: (b, 0, h)),
                  pl.BlockSpec((1, LANES, S), lambda b, h, i: (b, h, 0)),
                  pl.BlockSpec((4, HEAD_DIM), lambda b, h, i: (0, 0)),
                  pl.BlockSpec((1, DA_V_DIM), lambda b, h, i: (0, 0))],
        out_specs=pl.BlockSpec((1, tq, DA_V_DIM), lambda b, h, i: (b, i, h)),
        out_shape=jax.ShapeDtypeStruct((B, S, DA_WIDTH), F32),
        scratch_shapes=[pltpu.VMEM((2, 2, tk, tq), F32), pltpu.VMEM((2, DA_V_DIM, tq), F32)],
        compiler_params=pltpu.CompilerParams(
            dimension_semantics=("parallel", "parallel", "parallel"), vmem_limit_bytes=VMEM_LIMIT),
        name="diff_attn",
    )(qt, k, vt, lam_p, subln_g)


def _rwkv_prep(d, z_ref, pv_ref, wup_ref, aup_ref, gup_ref, seg):
    zs = z_ref[0]

    w = RW_WIDTH
    r, k, v = zs[:, 0:w], zs[:, w:2 * w], zs[:, 2 * w:3 * w]
    wa = zs[:, 3 * w:3 * w + LORA_IN]
    pv = lambda i: pv_ref[i:i + 1]
    w_pre = _dot(jnp.tanh(wa).astype(BF16), wup_ref[d])
    a_pre = _dot(wa.astype(BF16), aup_ref[d])
    lw = -math.exp(-0.5) * jax.nn.sigmoid(pv(PV_W0 + d) + w_pre)
    a_rate = jax.nn.sigmoid(pv(PV_A0 + d) + a_pre)
    kk = k * pv(PV_KK)
    kk = kk * lax.rsqrt(_seg_sum(kk * kk, seg) + KK_EPS)
    kd = k * (1.0 + (a_rate - 1.0) * pv(PV_KA))
    bonus = _seg_sum(r * kd * pv(PV_RK), seg) * v
    out = dict(r=r, v=v, kd=kd, lw=lw, a=-kk, b=kk * a_rate, bonus=bonus)
    if d == 0:
        g_dn = zs[:, 3 * w + LORA_IN:RW_COLS]
        out["g"] = _dot(jax.nn.sigmoid(g_dn).astype(BF16), gup_ref[...])
    return out


def _rwkv_chunks(cis, ops_ref, tri_ref, seg_ref, st_ref, y_refs):
    gw = SEG_WIDTH
    n_groups = RW_HEADS // GROUP_HEADS
    bmask = seg_ref[...]
    ti = lax.broadcasted_iota(jnp.int32, (CHUNK, gw), 0)
    si = lax.broadcasted_iota(jnp.int32, (CHUNK, gw), 1) % HEAD_DIM
    eye = (si == ti).astype(F32)
    strict = ((si < ti), (si > ti))
    incl = ((si <= ti), (si >= ti))
    nt = (((1,), (1,)), ((), ()))
    tn = (((0,), (0,)), ((), ()))
    lane = lax.broadcasted_iota(jnp.int32, (1, LANES), 1)
    half_masks = ((lane < HEAD_DIM).astype(BF16), (lane >= HEAD_DIM).astype(BF16))
    zeros = jnp.zeros((CHUNK, LANES), BF16)

    def blockdiag(x):
        xb = x.astype(BF16)
        blocks = []
        for h in range(GROUP_HEADS):
            part = xb[:, (h // 2) * LANES:(h // 2 + 1) * LANES] * half_masks[h % 2]
            blocks.append(jnp.concatenate([part, zeros] if h < 2 else [zeros, part], axis=1))
        return jnp.concatenate(blocks, axis=0)

    def stack(*xs):
        return jnp.concatenate([x.astype(BF16) for x in xs], axis=0)

    n_seq = len(cis[0])
    rows = [[pl.ds(pl.multiple_of(ci * CHUNK, CHUNK), CHUNK) for ci in cis[d]] for d in range(2)]
    chains = []
    for k in range(n_seq):
        for d in range(2):
            r, v, kd, lw, a, b = (ops_ref[d, n, rows[d][k], :] for n in range(6))
            cl = sum(_dot(tri_ref[d], part) for part in _split3(lw))
            tot = cl[CHUNK - 1:CHUNK] if d == 0 else cl[0:1]
            e_inv = jnp.exp(-cl)
            e_rem = jnp.exp(tot - cl)
            wide = dict(at=a * jnp.exp(cl - lw), rt=r * jnp.exp(cl), bt=b * e_inv, kt=kd * e_inv,
                        bh=b * e_rem, kh=kd * e_rem, v=v, decay=jnp.exp(tot))
            for g in range(n_groups):
                c = {name: x[:, g * gw:(g + 1) * gw] for name, x in wide.items()}
                c.update(k=k, d=d, g=g)
                chains.append(c)

    for c in chains:
        ar = stack(c["at"], c["rt"])
        c["ab"] = lax.dot_general(ar, blockdiag(c["bt"]), nt, preferred_element_type=F32)
        c["ak"] = lax.dot_general(ar, blockdiag(c["kt"]), nt, preferred_element_type=F32)
    for c in chains:
        d = c["d"]
        c["l"] = jnp.where(strict[d], c["ab"][:CHUNK], 0.0)
        c["a_rb"] = jnp.where(incl[d], c["ab"][CHUNK:], 0.0)
        a_ak = jnp.where(strict[d], c["ak"][:CHUNK], 0.0)
        a_rk = jnp.where(incl[d], c["ak"][CHUNK:], 0.0)
        av = _dot(stack(a_ak, a_rk), blockdiag(c["v"]))
        c["akv"], c["arkv"] = av[:CHUNK], av[CHUNK:]

    for c in chains:
        c["t"] = eye + c["l"]
        c["lp"] = _dot(c["l"].astype(BF16), blockdiag(c["l"]))
    for _ in range(int(math.log2(CHUNK)) - 2):
        for c in chains:
            both = _dot(stack(c["t"], c["lp"]), blockdiag(c["lp"]))
            c["t"] = c["t"] + both[:CHUNK]
            c["lp"] = both[CHUNK:]
    for c in chains:
        c["t"] = (c["t"] + _dot(c["t"].astype(BF16), blockdiag(c["lp"]))).astype(BF16)
    for c in chains:
        c["wt"] = _dot(c["t"], blockdiag(c["at"]))
        c["u_loc"] = _dot(c["t"], blockdiag(c["akv"]))

    state = {(d, g): st_ref[d, g] for d in range(2) for g in range(n_groups)}
    for k in range(n_seq):
        now = [c for c in chains if c["k"] == k]
        for c in now:
            wr = _dot(stack(c["wt"], c["rt"]), blockdiag(state[c["d"], c["g"]]))
            c["u"] = wr[:CHUNK] + c["u_loc"]
            c["y"] = wr[CHUNK:] + c["arkv"]
        for c in now:
            c["y"] = c["y"] + _dot(c["a_rb"].astype(BF16), blockdiag(c["u"]))
            st_hi, st_lo = _split2(state[c["d"], c["g"]])
            decay = eye * c["decay"]
            full = lax.dot_general(stack(c["bh"], c["kh"], decay, decay), stack(c["u"], c["v"], st_hi, st_lo),
                                   tn, preferred_element_type=F32) * bmask
            state[c["d"], c["g"]] = sum(full[h * HEAD_DIM:(h + 1) * HEAD_DIM] for h in range(GROUP_HEADS))
        for d in range(2):
            y_refs[d][0, rows[d][k], :] = jnp.concatenate([c["y"] for c in now if c["d"] == d], axis=1)
    for (d, g), st in state.items():
        st_ref[d, g] = st


def _rwkv_kernel(zf_ref, zb_ref, pv_ref, wup_ref, aup_ref, gup_ref, seg_ref, tri_ref,
                 y0_ref, y1_ref, bon0_ref, bon1_ref, g_ref, ops_ref, st_ref):
    tm = zf_ref.shape[1]

    @pl.when(pl.program_id(1) == 0)
    def _():
        st_ref[...] = jnp.zeros_like(st_ref)

    seg = seg_ref[...]
    for d, (z_ref, bon_ref) in enumerate(((zf_ref, bon0_ref), (zb_ref, bon1_ref))):
        p = _rwkv_prep(d, z_ref, pv_ref, wup_ref, aup_ref, gup_ref, seg)
        for n, name in enumerate(("r", "v", "kd", "lw", "a", "b")):
            ops_ref[d, n] = p[name]
        bon_ref[0] = p["bonus"]
        if d == 0:
            g_ref[0] = p["g"]

    n_chunks = tm // CHUNK
    per_iter = min(CHUNKS_PER_ITER, n_chunks)

    def body(it, carry):
        fwd = [it * per_iter + k for k in range(per_iter)]
        bwd = [n_chunks - 1 - c for c in fwd]
        _rwkv_chunks((fwd, bwd), ops_ref, tri_ref, seg_ref, st_ref, (y0_ref, y1_ref))
        return carry

    lax.fori_loop(0, n_chunks // per_iter, body, 0)


def _rwkv_scan(z, pvec, wup, aup, gup, seg, tri, tm):
    B, S, _ = z.shape
    nt = S // tm
    const = lambda shape: pl.BlockSpec(shape, lambda b, i: (0,) * len(shape))
    fwd = lambda width: pl.BlockSpec((1, tm, width), lambda b, i: (b, i, 0))
    bwd = lambda width: pl.BlockSpec((1, tm, width), lambda b, i: (b, nt - 1 - i, 0))
    tok = jax.ShapeDtypeStruct((B, S, RW_WIDTH), F32)
    return pl.pallas_call(
        _rwkv_kernel,
        grid=(B, nt),
        in_specs=[fwd(RW_COLS), bwd(RW_COLS), const((PV_ROWS, RW_WIDTH)),
                  const((2, LORA_IN, RW_WIDTH)), const((2, LORA_IN, RW_WIDTH)), const((GATE_LORA, RW_WIDTH)),
                  const((SEG_WIDTH, SEG_WIDTH)), const((2, CHUNK, CHUNK))],
        out_specs=[fwd(RW_WIDTH), bwd(RW_WIDTH), fwd(RW_WIDTH), bwd(RW_WIDTH), fwd(RW_WIDTH)],
        out_shape=[tok] * 5,
        scratch_shapes=[pltpu.VMEM((2, 6, tm, RW_WIDTH), F32),
                        pltpu.VMEM((2, RW_HEADS // GROUP_HEADS, HEAD_DIM, GROUP_HEADS * HEAD_DIM), F32)],
        compiler_params=pltpu.CompilerParams(
            dimension_semantics=("arbitrary", "arbitrary"), vmem_limit_bytes=VMEM_LIMIT),
        name="rwkv_scan",
    )(z, z, pvec, wup, aup, gup, seg, tri)


def _out_ffn_kernel(x_ref, oda_ref, y0_ref, y1_ref, bon0_ref, bon1_ref, g_ref, pv_ref, seg_ref,
                    wout_ref, g2_ref, w1_ref, w2_ref, o_ref, *, ff_chunk):
    seg = seg_ref[...]
    y = y0_ref[0] + y1_ref[0]
    mean = _seg_sum(y, seg) * (1.0 / HEAD_DIM)
    yc = y - mean
    var = _seg_sum(yc * yc, seg) * (1.0 / HEAD_DIM)
    yn = yc * lax.rsqrt(var + LN_X_EPS) * pv_ref[PV_LNG:PV_LNG + 1] + pv_ref[PV_LNB:PV_LNB + 1]
    o_rw = (yn + bon0_ref[0] + bon1_ref[0]) * g_ref[0]
    x = (x_ref[0] + _dot(oda_ref[0].astype(BF16), wout_ref[0:DA_WIDTH, :])
         + _dot(o_rw.astype(BF16), wout_ref[DA_WIDTH:D_MODEL, :]))
    ms = jnp.mean(x * x, axis=-1, keepdims=True)
    h = (x * lax.rsqrt(ms + NORM_EPS) * g2_ref[...]).astype(BF16)
    ffn = None
    for c in range(D_FF // ff_chunk):
        u = jnp.maximum(_dot(h, w1_ref[:, c * ff_chunk:(c + 1) * ff_chunk]), 0.0)
        part = _dot((u * u).astype(BF16), w2_ref[c * ff_chunk:(c + 1) * ff_chunk, :])
        ffn = part if ffn is None else ffn + part
    o_ref[0] = x + ffn


def _out_ffn(x, o_da, y0, y1, bon0, bon1, g, pvec, seg, w_out, g2, w1, w2, tm):
    B, S, _ = x.shape
    const = lambda shape: pl.BlockSpec(shape, lambda b, i: (0,) * len(shape))
    tok = lambda width: pl.BlockSpec((1, tm, width), lambda b, i: (b, i, 0))
    return pl.pallas_call(
        functools.partial(_out_ffn_kernel, ff_chunk=1024),
        grid=(B, S // tm),
        in_specs=[tok(D_MODEL)] + [tok(RW_WIDTH)] * 6 + [
            const((PV_ROWS, RW_WIDTH)), const((SEG_WIDTH, SEG_WIDTH)), const((D_MODEL, D_MODEL)),
            const((1, D_MODEL)), const((D_MODEL, D_FF)), const((D_FF, D_MODEL))],
        out_specs=tok(D_MODEL),
        out_shape=jax.ShapeDtypeStruct((B, S, D_MODEL), F32),
        compiler_params=pltpu.CompilerParams(
            dimension_semantics=("parallel", "parallel"), vmem_limit_bytes=VMEM_LIMIT),
        name="out_ffn",
    )(x, o_da, y0, y1, bon0, bon1, g, pvec, seg, w_out, g2, w1, w2)


def _rope_tables(seq_len):
    inv_freq = 1.0 / (ROPE_THETA ** (jnp.arange(0, HEAD_DIM, 2, dtype=F32) / HEAD_DIM))
    ang = jnp.arange(seq_len, dtype=F32)[:, None] * inv_freq[None, :]
    cos, sin = jnp.cos(ang), jnp.sin(ang)
    reps = LANES // HEAD_DIM
    cos_t = jnp.concatenate([cos, cos] * reps, axis=-1)
    sin_t = jnp.concatenate([-sin, sin] * reps, axis=-1)
    return cos_t, sin_t


def _attn_tiles(seq_len):
    if seq_len >= 8192:
        return 256, 1024, 4
    tk = _pick(seq_len // 2, 256)
    return _pick(seq_len, 1024), tk, 2


def _pick(n, target):
    t = min(n, target)
    assert n % t == 0, (n, t)
    return t


def kernel(x_prompt, x_sample, norm1_g, w_in, q_norm_g, k_norm_g, lam_q1, lam_k1, lam_q2, lam_k2, subln_g,
           mu_prev, mu_next, w0, w_up, a0, a_up, g_up, k_k, k_a, r_k, ln_x_g, ln_x_b, w_out, norm2_g,
           w_ff1, w_ff2):
    l = 0
    w_in_b = w_in[l].astype(BF16)
    w_out_b = w_out[l].astype(BF16)
    w1_b = w_ff1[l].astype(BF16)
    w2_b = w_ff2[l].astype(BF16)
    g1 = norm1_g[l][None, :]
    g2 = norm2_g[l][None, :]
    qg = jnp.tile(q_norm_g[l], DA_WIDTH // HEAD_DIM)[None, :]
    kg = jnp.tile(k_norm_g[l], DA_WIDTH // HEAD_DIM)[None, :]
    lam_p = jnp.stack([lam_q1[l], lam_k1[l], lam_q2[l], lam_k2[l]])
    sg = subln_g[l][None, :]
    mu = jnp.stack([mu_prev[l], mu_next[l]])
    rows = [w0[l, 0], w0[l, 1], a0[l, 0], a0[l, 1], k_k[l], k_a[l], r_k[l].reshape(-1), ln_x_g[l], ln_x_b[l]]
    pvec = jnp.zeros((PV_ROWS, RW_WIDTH), F32).at[:len(rows)].set(jnp.stack(rows))
    zpad = jnp.zeros((2, DECAY_LORA, RW_WIDTH), F32)
    wup = jnp.concatenate([w_up[l], zpad], axis=1).astype(BF16)
    aup = jnp.concatenate([zpad, a_up[l]], axis=1).astype(BF16)
    gup = g_up[l].astype(BF16)
    ch = jnp.arange(SEG_WIDTH) // HEAD_DIM
    seg = (ch[:, None] == ch[None, :]).astype(BF16)
    t = jnp.arange(CHUNK)
    tri = jnp.stack([t[None, :] <= t[:, None], t[None, :] >= t[:, None]]).astype(BF16)

    def run(x):
        S = x.shape[1]
        cos_t, sin_t = _rope_tables(S)
        qt, k, vt, z_rw = _in_proj(x, g1, w_in_b, qg, kg, cos_t, sin_t, seg, mu, _pick(S, 512))
        tq, tk, per_iter = _attn_tiles(S)
        o_da = _diff_attn(qt, k, vt, lam_p, sg, tq, tk, per_iter)
        y0, y1, bon0, bon1, g = _rwkv_scan(z_rw, pvec, wup, aup, gup, seg, tri, _pick(S, 256))
        return _out_ffn(x, o_da, y0, y1, bon0, bon1, g, pvec, seg, w_out_b, g2, w1_b, w2_b, _pick(S, 512))

    return (run(x_prompt), run(x_sample))
```

```python
import functools
import math

import jax
import jax.numpy as jnp
from jax import lax
from jax.experimental import pallas as pl
from jax.experimental.pallas import tpu as pltpu

F32 = jnp.float32
BF16 = jnp.bfloat16

D_MODEL = 1024
DA_HEADS = 4
HEAD_DIM = 64
DA_V_DIM = 128
DA_WIDTH = DA_HEADS * DA_V_DIM
RW_WIDTH = D_MODEL - DA_WIDTH
RW_HEADS = RW_WIDTH // HEAD_DIM
DECAY_LORA = 64
ICLR_LORA = 64
GATE_LORA = 128
LORA_IN = DECAY_LORA + ICLR_LORA
RW_COLS = 3 * RW_WIDTH + LORA_IN + GATE_LORA
DA_COLS = 3 * DA_WIDTH
IN_COLS = DA_COLS + RW_COLS
D_FF = 4 * D_MODEL
ROPE_THETA = 10000.0
NORM_EPS = 1e-6
LN_X_EPS = 64e-5
KK_EPS = 1e-12
LAMBDA_INIT = 0.8 - 0.6 * math.exp(-0.3 * 0)
LOG2E = 1.4426950408889634
QK_SCALE = HEAD_DIM ** -0.5

LANES = 128
CHUNK = 64
GROUP_HEADS = 4
SEG_WIDTH = GROUP_HEADS * HEAD_DIM
CHUNKS_PER_ITER = 4
VMEM_LIMIT = 56 * 1024 * 1024

PV_W0, PV_A0, PV_KK, PV_KA, PV_RK, PV_LNG, PV_LNB = 0, 2, 4, 5, 6, 7, 8
PV_ROWS = 16


def _dot(a, b):
    return jnp.dot(a, b, preferred_element_type=F32)


def _split2(x):
    hi = x.astype(BF16)
    lo = (x - hi.astype(F32)).astype(BF16)
    return hi, lo


def _split3(x):
    hi = x.astype(BF16)
    r1 = x - hi.astype(F32)
    mid = r1.astype(BF16)
    lo = (r1 - mid.astype(F32)).astype(BF16)
    return hi, mid, lo


def _seg_sum(x, seg):
    xb = x.astype(BF16)
    gw = seg.shape[0]
    return jnp.concatenate([_dot(xb[:, g * gw:(g + 1) * gw], seg) for g in range(x.shape[1] // gw)], axis=1)


def _in_proj_kernel(x_ref, xp_ref, xn_ref, g1_ref, w_ref, qg_ref, kg_ref, cos_ref, sin_ref, seg_ref, mu_ref,
                    qt_ref, k_ref, vt_ref, z_ref):
    def norm1(x):
        ms = jnp.mean(x * x, axis=-1, keepdims=True)
        return (x * lax.rsqrt(ms + NORM_EPS) * g1_ref[...]).astype(BF16)

    h = norm1(x_ref[0])
    tm = h.shape[0]
    seg = seg_ref[...]
    reps = DA_WIDTH // LANES
    cos = jnp.concatenate([cos_ref[...]] * reps, axis=1)
    sin = jnp.concatenate([sin_ref[...]] * reps, axis=1)
    lane = lax.broadcasted_iota(jnp.int32, (1, LANES), 1)
    first_half = (lane % HEAD_DIM) < (HEAD_DIM // 2)

    def head_norm_rope(z, g):
        ss = _seg_sum(z * z, seg) * (1.0 / HEAD_DIM)
        zn = z * lax.rsqrt(ss + NORM_EPS) * g
        parts = []
        for c in range(reps):
            zc = zn[:, c * LANES:(c + 1) * LANES]
            parts.append(jnp.where(first_half,
                                   pltpu.roll(zc, LANES - HEAD_DIM // 2, 1),
                                   pltpu.roll(zc, HEAD_DIM // 2, 1)))
        rot = jnp.concatenate(parts, axis=1)
        return zn * cos + rot * sin

    zq = _dot(h, w_ref[:, 0:DA_WIDTH])
    qt_ref[0] = (head_norm_rope(zq, qg_ref[...]) * (QK_SCALE * LOG2E)).T.astype(BF16)
    zk = _dot(h, w_ref[:, DA_WIDTH:2 * DA_WIDTH])
    k_ref[0] = head_norm_rope(zk, kg_ref[...]).astype(BF16)
    vt_ref[0] = _dot(h, w_ref[:, 2 * DA_WIDTH:DA_COLS]).T.astype(BF16)

    i = pl.program_id(1)
    h_ext = jnp.concatenate([h, norm1(xp_ref[0]), norm1(xn_ref[0])], axis=0)
    row = lax.broadcasted_iota(jnp.int32, (tm, 1), 0)
    slab = lambda c0: _dot(h_ext, w_ref[:, DA_COLS + c0:DA_COLS + c0 + SEG_WIDTH])
    z_next_group = slab(0)
    for c0 in range(0, RW_COLS, SEG_WIDTH):
        cols = slice(c0, c0 + SEG_WIDTH)
        z_ext = z_next_group
        if c0 + SEG_WIDTH < RW_COLS:
            z_next_group = slab(c0 + SEG_WIDTH)
        z = z_ext[:tm]
        prev_row = jnp.where(i > 0, z_ext[tm + 7:tm + 8], 0.0)
        next_row = jnp.where(i < pl.num_programs(1) - 1, z_ext[tm + 8:tm + 9], 0.0)
        z_prev = jnp.where(row == 0, prev_row, pltpu.roll(z, 1, 0))
        z_next = jnp.where(row == tm - 1, next_row, pltpu.roll(z, tm - 1, 0))
        z_ref[0, :, cols] = z + mu_ref[0:1, cols] * (z_prev - z) + mu_ref[1:2, cols] * (z_next - z)


def _in_proj(x, g1, w_in, qg, kg, cos_t, sin_t, seg, mu, tm):
    B, S, _ = x.shape
    const = lambda shape: pl.BlockSpec(shape, lambda b, i: (0,) * len(shape))
    tok = lambda width: pl.BlockSpec((1, tm, width), lambda b, i: (b, i, 0))
    tok_t = pl.BlockSpec((1, DA_WIDTH, tm), lambda b, i: (b, 0, i))
    halo_prev = pl.BlockSpec((1, 8, D_MODEL), lambda b, i: (b, jnp.maximum(i * (tm // 8) - 1, 0), 0))
    halo_next = pl.BlockSpec((1, 8, D_MODEL), lambda b, i: (b, jnp.minimum((i + 1) * (tm // 8), S // 8 - 1), 0))
    return pl.pallas_call(
        _in_proj_kernel,
        grid=(B, S // tm),
        in_specs=[tok(D_MODEL), halo_prev, halo_next, const((1, D_MODEL)), const((D_MODEL, IN_COLS)),
                  const((1, DA_WIDTH)), const((1, DA_WIDTH)),
                  pl.BlockSpec((tm, LANES), lambda b, i: (i, 0)),
                  pl.BlockSpec((tm, LANES), lambda b, i: (i, 0)),
                  const((SEG_WIDTH, SEG_WIDTH)), const((2, RW_COLS))],
        out_specs=[tok_t, tok(DA_WIDTH), tok_t, tok(RW_COLS)],
        out_shape=[jax.ShapeDtypeStruct((B, DA_WIDTH, S), BF16), jax.ShapeDtypeStruct((B, S, DA_WIDTH), BF16),
                   jax.ShapeDtypeStruct((B, DA_WIDTH, S), BF16), jax.ShapeDtypeStruct((B, S, RW_COLS), F32)],
        compiler_params=pltpu.CompilerParams(
            dimension_semantics=("parallel", "parallel"), vmem_limit_bytes=VMEM_LIMIT),
        name="in_proj",
    )(x, x, x, g1, w_in, qg, kg, cos_t, sin_t, seg, mu)


def _diff_attn_kernel(qt_ref, qtn_ref, k_ref, vt_ref, lam_ref, sg_ref, o_ref, s_ref, acc_ref, mb_ref, *, tk,
                      per_iter):
    tq = qt_ref.shape[2]
    nk = k_ref.shape[1] // tk
    row = lax.broadcasted_iota(jnp.int32, (LANES, 1), 0)

    def components(qt):
        zero = jnp.zeros_like(qt)
        return jnp.where(row < HEAD_DIM, qt, zero), jnp.where(row >= HEAD_DIM, qt, zero)

    qts = components(qt_ref[0])
    acc_ref[...] = jnp.zeros_like(acc_ref)

    def scores(slot, blk, q_pair):
        kb = k_ref[0, pl.ds(pl.multiple_of(blk * tk, tk), tk), :]
        mblk = []
        for c in range(2):
            s = _dot(kb, q_pair[c])
            s_ref[slot, c] = s
            mblk.append(jnp.max(s, axis=0, keepdims=True))
        return tuple(mblk)

    @pl.when(pl.program_id(2) == 0)
    def _():
        first = scores(0, 0, qts)
        for c in range(2):
            mb_ref[c] = first[c]

    def consume(slot, blk, mblk, ml):
        vtb = vt_ref[0, :, pl.ds(pl.multiple_of(blk * tk, tk), tk)]
        out = []
        for c in range(2):
            m, l = ml[c]
            m_new = jnp.maximum(m, mblk[c])
            alpha = jnp.exp2(m - m_new)
            p = jnp.exp2(s_ref[slot, c] - m_new)
            l = alpha * l + jnp.sum(p, axis=0, keepdims=True)
            acc_ref[c] = alpha * acc_ref[c] + _dot(vtb, p.astype(BF16))
            out.append((m_new, l))
        return tuple(out)

    def steps(first, carry, last):
        mblk, ml = carry
        for u in range(per_iter):
            if last and u == per_iter - 1:
                nxt = scores(0, 0, components(qtn_ref[0]))
            else:
                nxt = scores((u + 1) % 2, first + u + 1, qts)
            ml = consume(u % 2, first + u, mblk, ml)
            mblk = nxt
        return mblk, ml

    ml = tuple((jnp.full((1, tq), -1e30, F32), jnp.zeros((1, tq), F32)) for _ in range(2))
    carry = lax.fori_loop(0, nk // per_iter - 1, lambda i, c: steps(i * per_iter, c, False),
                          ((mb_ref[0], mb_ref[1]), ml))
    mb_next, ((_, l0), (_, l1)) = steps(nk - per_iter, carry, True)
    for c in range(2):
        mb_ref[c] = mb_next[c]

    lp = lam_ref[...]
    lam = (jnp.exp(jnp.sum(lp[0:1] * lp[1:2], axis=1, keepdims=True))
           - jnp.exp(jnp.sum(lp[2:3] * lp[3:4], axis=1, keepdims=True)) + LAMBDA_INIT)
    o = (acc_ref[0] / l0 - lam * (acc_ref[1] / l1)).T
    ms = jnp.mean(o * o, axis=-1, keepdims=True)
    o_ref[0] = o * lax.rsqrt(ms + NORM_EPS) * sg_ref[...] * (1.0 - LAMBDA_INIT)


def _diff_attn(qt, k, vt, lam_p, subln_g, tq, tk, per_iter):
    B, S, _ = k.shape
    assert per_iter % 2 == 0 and (S // tk) % per_iter == 0
    nq = S // tq
    return pl.pallas_call(
        functools.partial(_diff_attn_kernel, tk=tk, per_iter=per_iter),
        grid=(B, DA_HEADS, nq),
        in_specs=[pl.BlockSpec((1, LANES, tq), lambda b, h, i: (b, h, i)),
                  pl.BlockSpec((1, LANES, tq), lambda b, h, i: (b, h, jnp.minimum(i + 1, nq - 1))),
                  pl.BlockSpec((1, S, LANES), lambda b, h, i: (b, 0, h)),
                  pl.BlockSpec((1, LANES, S), lambda b, h, i: (b, h, 0)),
                  pl.BlockSpec((4, HEAD_DIM), lambda b, h, i: (0, 0)),
                  pl.BlockSpec((1, DA_V_DIM), lambda b, h, i: (0, 0))],
        out_specs=pl.BlockSpec((1, tq, DA_V_DIM), lambda b, h, i: (b, i, h)),
        out_shape=jax.ShapeDtypeStruct((B, S, DA_WIDTH), F32),
        scratch_shapes=[pltpu.VMEM((2, 2, tk, tq), F32), pltpu.VMEM((2, DA_V_DIM, tq), F32),
                        pltpu.VMEM((2, 1, tq), F32)],
        compiler_params=pltpu.CompilerParams(
            dimension_semantics=("parallel", "parallel", "arbitrary"), vmem_limit_bytes=VMEM_LIMIT),
        name="diff_attn",
    )(qt, qt, k, vt, lam_p, subln_g)


def _rwkv_prep(d, z_ref, pv_ref, wup_ref, aup_ref, gup_ref, seg):
    zs = z_ref[0]

    w = RW_WIDTH
    r, k, v = zs[:, 0:w], zs[:, w:2 * w], zs[:, 2 * w:3 * w]
    wa = zs[:, 3 * w:3 * w + LORA_IN]
    pv = lambda i: pv_ref[i:i + 1]
    w_pre = _dot(jnp.tanh(wa).astype(BF16), wup_ref[d])
    a_pre = _dot(wa.astype(BF16), aup_ref[d])
    lw = -math.exp(-0.5) * jax.nn.sigmoid(pv(PV_W0 + d) + w_pre)
    a_rate = jax.nn.sigmoid(pv(PV_A0 + d) + a_pre)
    kk = k * pv(PV_KK)
    kk = kk * lax.rsqrt(_seg_sum(kk * kk, seg) + KK_EPS)
    kd = k * (1.0 + (a_rate - 1.0) * pv(PV_KA))
    bonus = _seg_sum(r * kd * pv(PV_RK), seg) * v
    out = dict(r=r, v=v, kd=kd, lw=lw, a=-kk, b=kk * a_rate, bonus=bonus)
    if d == 0:
        g_dn = zs[:, 3 * w + LORA_IN:RW_COLS]
        out["g"] = _dot(jax.nn.sigmoid(g_dn).astype(BF16), gup_ref[...])
    return out


def _rwkv_chunks(cis, ops_ref, tri_ref, seg_ref, st_ref, y_refs):
    gw = SEG_WIDTH
    n_groups = RW_HEADS // GROUP_HEADS
    bmask = seg_ref[...]
    ti = lax.broadcasted_iota(jnp.int32, (CHUNK, gw), 0)
    si = lax.broadcasted_iota(jnp.int32, (CHUNK, gw), 1) % HEAD_DIM
    eye = (si == ti).astype(F32)
    strict = ((si < ti), (si > ti))
    incl = ((si <= ti), (si >= ti))
    nt = (((1,), (1,)), ((), ()))
    tn = (((0,), (0,)), ((), ()))
    lane = lax.broadcasted_iota(jnp.int32, (1, LANES), 1)
    half_masks = ((lane < HEAD_DIM).astype(BF16), (lane >= HEAD_DIM).astype(BF16))
    zeros = jnp.zeros((CHUNK, LANES), BF16)

    def blockdiag(x):
        xb = x.astype(BF16)
        blocks = []
        for h in range(GROUP_HEADS):
            part = xb[:, (h // 2) * LANES:(h // 2 + 1) * LANES] * half_masks[h % 2]
            blocks.append(jnp.concatenate([part, zeros] if h < 2 else [zeros, part], axis=1))
        return jnp.concatenate(blocks, axis=0)

    def stack(*xs):
        return jnp.concatenate([x.astype(BF16) for x in xs], axis=0)

    n_seq = len(cis[0])
    rows = [[pl.ds(pl.multiple_of(ci * CHUNK, CHUNK), CHUNK) for ci in cis[d]] for d in range(2)]
    chains = []
    for k in range(n_seq):
        for d in range(2):
            r, v, kd, lw, a, b = (ops_ref[d, n, rows[d][k], :] for n in range(6))
            cl = sum(_dot(tri_ref[d], part) for part in _split3(lw))
            tot = cl[CHUNK - 1:CHUNK] if d == 0 else cl[0:1]
            e_inv = jnp.exp(-cl)
            e_rem = jnp.exp(tot - cl)
            wide = dict(at=a * jnp.exp(cl - lw), rt=r * jnp.exp(cl), bt=b * e_inv, kt=kd * e_inv,
                        bh=b * e_rem, kh=kd * e_rem, v=v, decay=jnp.exp(tot))
            for g in range(n_groups):
                c = {name: x[:, g * gw:(g + 1) * gw] for name, x in wide.items()}
                c.update(k=k, d=d, g=g)
                chains.append(c)

    for c in chains:
        ar = stack(c["at"], c["rt"])
        c["ab"] = lax.dot_general(ar, blockdiag(c["bt"]), nt, preferred_element_type=F32)
        c["ak"] = lax.dot_general(ar, blockdiag(c["kt"]), nt, preferred_element_type=F32)
    for c in chains:
        d = c["d"]
        c["l"] = jnp.where(strict[d], c["ab"][:CHUNK], 0.0)
        c["a_rb"] = jnp.where(incl[d], c["ab"][CHUNK:], 0.0)
        a_ak = jnp.where(strict[d], c["ak"][:CHUNK], 0.0)
        a_rk = jnp.where(incl[d], c["ak"][CHUNK:], 0.0)
        av = _dot(stack(a_ak, a_rk), blockdiag(c["v"]))
        c["akv"], c["arkv"] = av[:CHUNK], av[CHUNK:]

    for c in chains:
        c["t"] = eye + c["l"]
        c["lp"] = _dot(c["l"].astype(BF16), blockdiag(c["l"]))
    for _ in range(int(math.log2(CHUNK)) - 2):
        for c in chains:
            both = _dot(stack(c["t"], c["lp"]), blockdiag(c["lp"]))
            c["t"] = c["t"] + both[:CHUNK]
            c["lp"] = both[CHUNK:]
    for c in chains:
        c["t"] = (c["t"] + _dot(c["t"].astype(BF16), blockdiag(c["lp"]))).astype(BF16)
    for c in chains:
        c["wt"] = _dot(c["t"], blockdiag(c["at"]))
        c["u_loc"] = _dot(c["t"], blockdiag(c["akv"]))

    state = {(d, g): st_ref[d, g] for d in range(2) for g in range(n_groups)}
    for k in range(n_seq):
        now = [c for c in chains if c["k"] == k]
        for c in now:
            wr = _dot(stack(c["wt"], c["rt"]), blockdiag(state[c["d"], c["g"]]))
            c["u"] = wr[:CHUNK] + c["u_loc"]
            c["y"] = wr[CHUNK:] + c["arkv"]
        for c in now:
            c["y"] = c["y"] + _dot(c["a_rb"].astype(BF16), blockdiag(c["u"]))
            st_hi, st_lo = _split2(state[c["d"], c["g"]])
            decay = eye * c["decay"]
            full = lax.dot_general(stack(c["bh"], c["kh"], decay, decay), stack(c["u"], c["v"], st_hi, st_lo),
                                   tn, preferred_element_type=F32) * bmask
            state[c["d"], c["g"]] = sum(full[h * HEAD_DIM:(h + 1) * HEAD_DIM] for h in range(GROUP_HEADS))
        for d in range(2):
            y_refs[d][0, rows[d][k], :] = jnp.concatenate([c["y"] for c in now if c["d"] == d], axis=1)
    for (d, g), st in state.items():
        st_ref[d, g] = st


def _rwkv_kernel(zf_ref, zb_ref, pv_ref, wup_ref, aup_ref, gup_ref, seg_ref, tri_ref,
                 y0_ref, y1_ref, bon0_ref, bon1_ref, g_ref, ops_ref, st_ref):
    tm = zf_ref.shape[1]

    @pl.when(pl.program_id(1) == 0)
    def _():
        st_ref[...] = jnp.zeros_like(st_ref)

    seg = seg_ref[...]
    for d, (z_ref, bon_ref) in enumerate(((zf_ref, bon0_ref), (zb_ref, bon1_ref))):
        p = _rwkv_prep(d, z_ref, pv_ref, wup_ref, aup_ref, gup_ref, seg)
        for n, name in enumerate(("r", "v", "kd", "lw", "a", "b")):
            ops_ref[d, n] = p[name]
        bon_ref[0] = p["bonus"]
        if d == 0:
            g_ref[0] = p["g"]

    n_chunks = tm // CHUNK
    per_iter = min(CHUNKS_PER_ITER, n_chunks)

    def body(it, carry):
        fwd = [it * per_iter + k for k in range(per_iter)]
        bwd = [n_chunks - 1 - c for c in fwd]
        _rwkv_chunks((fwd, bwd), ops_ref, tri_ref, seg_ref, st_ref, (y0_ref, y1_ref))
        return carry

    lax.fori_loop(0, n_chunks // per_iter, body, 0)


def _rwkv_scan(z, pvec, wup, aup, gup, seg, tri, tm):
    B, S, _ = z.shape
    nt = S // tm
    const = lambda shape: pl.BlockSpec(shape, lambda b, i: (0,) * len(shape))
    fwd = lambda width: pl.BlockSpec((1, tm, width), lambda b, i: (b, i, 0))
    bwd = lambda width: pl.BlockSpec((1, tm, width), lambda b, i: (b, nt - 1 - i, 0))
    tok = jax.ShapeDtypeStruct((B, S, RW_WIDTH), F32)
    return pl.pallas_call(
        _rwkv_kernel,
        grid=(B, nt),
        in_specs=[fwd(RW_COLS), bwd(RW_COLS), const((PV_ROWS, RW_WIDTH)),
                  const((2, LORA_IN, RW_WIDTH)), const((2, LORA_IN, RW_WIDTH)), const((GATE_LORA, RW_WIDTH)),
                  const((SEG_WIDTH, SEG_WIDTH)), const((2, CHUNK, CHUNK))],
        out_specs=[fwd(RW_WIDTH), bwd(RW_WIDTH), fwd(RW_WIDTH), bwd(RW_WIDTH), fwd(RW_WIDTH)],
        out_shape=[tok] * 5,
        scratch_shapes=[pltpu.VMEM((2, 6, tm, RW_WIDTH), F32),
                        pltpu.VMEM((2, RW_HEADS // GROUP_HEADS, HEAD_DIM, GROUP_HEADS * HEAD_DIM), F32)],
        compiler_params=pltpu.CompilerParams(
            dimension_semantics=("arbitrary", "arbitrary"), vmem_limit_bytes=VMEM_LIMIT),
        name="rwkv_scan",
    )(z, z, pvec, wup, aup, gup, seg, tri)


def _out_ffn_kernel(x_ref, oda_ref, y0_ref, y1_ref, bon0_ref, bon1_ref, g_ref, pv_ref, seg_ref,
                    wout_ref, g2_ref, w1_ref, w2_ref, o_ref, *, ff_chunk):
    seg = seg_ref[...]
    y = y0_ref[0] + y1_ref[0]
    mean = _seg_sum(y, seg) * (1.0 / HEAD_DIM)
    yc = y - mean
    var = _seg_sum(yc * yc, seg) * (1.0 / HEAD_DIM)
    yn = yc * lax.rsqrt(var + LN_X_EPS) * pv_ref[PV_LNG:PV_LNG + 1] + pv_ref[PV_LNB:PV_LNB + 1]
    o_rw = (yn + bon0_ref[0] + bon1_ref[0]) * g_ref[0]
    x = (x_ref[0] + _dot(oda_ref[0].astype(BF16), wout_ref[0:DA_WIDTH, :])
         + _dot(o_rw.astype(BF16), wout_ref[DA_WIDTH:D_MODEL, :]))
    ms = jnp.mean(x * x, axis=-1, keepdims=True)
    h = (x * lax.rsqrt(ms + NORM_EPS) * g2_ref[...]).astype(BF16)
    ffn = None
    for c in range(D_FF // ff_chunk):
        u = jnp.maximum(_dot(h, w1_ref[:, c * ff_chunk:(c + 1) * ff_chunk]), 0.0)
        part = _dot((u * u).astype(BF16), w2_ref[c * ff_chunk:(c + 1) * ff_chunk, :])
        ffn = part if ffn is None else ffn + part
    o_ref[0] = x + ffn


def _out_ffn(x, o_da, y0, y1, bon0, bon1, g, pvec, seg, w_out, g2, w1, w2, tm):
    B, S, _ = x.shape
    const = lambda shape: pl.BlockSpec(shape, lambda b, i: (0,) * len(shape))
    tok = lambda width: pl.BlockSpec((1, tm, width), lambda b, i: (b, i, 0))
    return pl.pallas_call(
        functools.partial(_out_ffn_kernel, ff_chunk=1024),
        grid=(B, S // tm),
        in_specs=[tok(D_MODEL)] + [tok(RW_WIDTH)] * 6 + [
            const((PV_ROWS, RW_WIDTH)), const((SEG_WIDTH, SEG_WIDTH)), const((D_MODEL, D_MODEL)),
            const((1, D_MODEL)), const((D_MODEL, D_FF)), const((D_FF, D_MODEL))],
        out_specs=tok(D_MODEL),
        out_shape=jax.ShapeDtypeStruct((B, S, D_MODEL), F32),
        compiler_params=pltpu.CompilerParams(
            dimension_semantics=("parallel", "parallel"), vmem_limit_bytes=VMEM_LIMIT),
        name="out_ffn",
    )(x, o_da, y0, y1, bon0, bon1, g, pvec, seg, w_out, g2, w1, w2)


def _rope_tables(seq_len):
    inv_freq = 1.0 / (ROPE_THETA ** (jnp.arange(0, HEAD_DIM, 2, dtype=F32) / HEAD_DIM))
    ang = jnp.arange(seq_len, dtype=F32)[:, None] * inv_freq[None, :]
    cos, sin = jnp.cos(ang), jnp.sin(ang)
    reps = LANES // HEAD_DIM
    cos_t = jnp.concatenate([cos, cos] * reps, axis=-1)
    sin_t = jnp.concatenate([-sin, sin] * reps, axis=-1)
    return cos_t, sin_t


def _attn_tiles(seq_len):
    if seq_len >= 8192:
        return 256, 1024, 4
    tk = _pick(seq_len // 2, 256)
    return _pick(seq_len, 1024), tk, 2


def _pick(n, target):
    t = min(n, target)
    assert n % t == 0, (n, t)
    return t


def kernel(x_prompt, x_sample, norm1_g, w_in, q_norm_g, k_norm_g, lam_q1, lam_k1, lam_q2, lam_k2, subln_g,
           mu_prev, mu_next, w0, w_up, a0, a_up, g_up, k_k, k_a, r_k, ln_x_g, ln_x_b, w_out, norm2_g,
           w_ff1, w_ff2):
    l = 0
    w_in_b = w_in[l].astype(BF16)
    w_out_b = w_out[l].astype(BF16)
    w1_b = w_ff1[l].astype(BF16)
    w2_b = w_ff2[l].astype(BF16)
    g1 = norm1_g[l][None, :]
    g2 = norm2_g[l][None, :]
    qg = jnp.tile(q_norm_g[l], DA_WIDTH // HEAD_DIM)[None, :]
    kg = jnp.tile(k_norm_g[l], DA_WIDTH // HEAD_DIM)[None, :]
    lam_p = jnp.stack([lam_q1[l], lam_k1[l], lam_q2[l], lam_k2[l]])
    sg = subln_g[l][None, :]
    mu = jnp.stack([mu_prev[l], mu_next[l]])
    rows = [w0[l, 0], w0[l, 1], a0[l, 0], a0[l, 1], k_k[l], k_a[l], r_k[l].reshape(-1), ln_x_g[l], ln_x_b[l]]
    pvec = jnp.zeros((PV_ROWS, RW_WIDTH), F32).at[:len(rows)].set(jnp.stack(rows))
    zpad = jnp.zeros((2, DECAY_LORA, RW_WIDTH), F32)
    wup = jnp.concatenate([w_up[l], zpad], axis=1).astype(BF16)
    aup = jnp.concatenate([zpad, a_up[l]], axis=1).astype(BF16)
    gup = g_up[l].astype(BF16)
    ch = jnp.arange(SEG_WIDTH) // HEAD_DIM
    seg = (ch[:, None] == ch[None, :]).astype(BF16)
    t = jnp.arange(CHUNK)
    tri = jnp.stack([t[None, :] <= t[:, None], t[None, :] >= t[:, None]]).astype(BF16)

    def run(x):
        S = x.shape[1]
        cos_t, sin_t = _rope_tables(S)
        qt, k, vt, z_rw = _in_proj(x, g1, w_in_b, qg, kg, cos_t, sin_t, seg, mu, _pick(S, 512))
        tq, tk, per_iter = _attn_tiles(S)
        o_da = _diff_attn(qt, k, vt, lam_p, sg, tq, tk, per_iter)
        y0, y1, bon0, bon1, g = _rwkv_scan(z_rw, pvec, wup, aup, gup, seg, tri, _pick(S, 256))
        return _out_ffn(x, o_da, y0, y1, bon0, bon1, g, pvec, seg, w_out_b, g2, w1_b, w2_b, _pick(S, 512))

    return (run(x_prompt), run(x_sample))
```

```python
import functools
import math

import jax
import jax.numpy as jnp
from jax import lax
from jax.experimental import pallas as pl
from jax.experimental.pallas import tpu as pltpu

F32 = jnp.float32
BF16 = jnp.bfloat16

D_MODEL = 1024
DA_HEADS = 4
HEAD_DIM = 64
DA_V_DIM = 128
DA_WIDTH = DA_HEADS * DA_V_DIM
RW_WIDTH = D_MODEL - DA_WIDTH
RW_HEADS = RW_WIDTH // HEAD_DIM
DECAY_LORA = 64
ICLR_LORA = 64
GATE_LORA = 128
LORA_IN = DECAY_LORA + ICLR_LORA
RW_COLS = 3 * RW_WIDTH + LORA_IN + GATE_LORA
DA_COLS = 3 * DA_WIDTH
IN_COLS = DA_COLS + RW_COLS
D_FF = 4 * D_MODEL
ROPE_THETA = 10000.0
NORM_EPS = 1e-6
LN_X_EPS = 64e-5
KK_EPS = 1e-12
LAMBDA_INIT = 0.8 - 0.6 * math.exp(-0.3 * 0)
LOG2E = 1.4426950408889634
QK_SCALE = HEAD_DIM ** -0.5

LANES = 128
BF16_SUBLANES = 16
CHUNK = 64
GROUP_HEADS = 4
SEG_WIDTH = GROUP_HEADS * HEAD_DIM
CHUNKS_PER_ITER = 4
VMEM_LIMIT = 56 * 1024 * 1024

PV_W0, PV_A0, PV_KK, PV_KA, PV_RK, PV_LNG, PV_LNB = 0, 2, 4, 5, 6, 7, 8
PV_ROWS = 16


def _dot(a, b):
    return jnp.dot(a, b, preferred_element_type=F32)


def _split2(x):
    hi = x.astype(BF16)
    lo = (x - hi.astype(F32)).astype(BF16)
    return hi, lo


def _split3(x):
    hi = x.astype(BF16)
    r1 = x - hi.astype(F32)
    mid = r1.astype(BF16)
    lo = (r1 - mid.astype(F32)).astype(BF16)
    return hi, mid, lo


def _seg_sum(x, seg):
    xb = x.astype(BF16)
    gw = seg.shape[0]
    return jnp.concatenate([_dot(xb[:, g * gw:(g + 1) * gw], seg) for g in range(x.shape[1] // gw)], axis=1)


def _in_proj_kernel(x_ref, xp_ref, xn_ref, g1_ref, w_ref, qg_ref, kg_ref, cos_ref, sin_ref, seg_ref, mu_ref,
                    qt_ref, k_ref, vt_ref, z_ref):
    def norm1(x):
        ms = jnp.mean(x * x, axis=-1, keepdims=True)
        return (x * lax.rsqrt(ms + NORM_EPS) * g1_ref[...]).astype(BF16)

    h = norm1(x_ref[0])
    tm = h.shape[0]
    seg = seg_ref[...]
    reps = DA_WIDTH // LANES
    cos = jnp.concatenate([cos_ref[...]] * reps, axis=1)
    sin = jnp.concatenate([sin_ref[...]] * reps, axis=1)
    lane = lax.broadcasted_iota(jnp.int32, (1, LANES), 1)
    first_half = (lane % HEAD_DIM) < (HEAD_DIM // 2)

    def head_norm_rope(z, g):
        ss = _seg_sum(z * z, seg) * (1.0 / HEAD_DIM)
        zn = z * lax.rsqrt(ss + NORM_EPS) * g
        parts = []
        for c in range(reps):
            zc = zn[:, c * LANES:(c + 1) * LANES]
            parts.append(jnp.where(first_half,
                                   pltpu.roll(zc, LANES - HEAD_DIM // 2, 1),
                                   pltpu.roll(zc, HEAD_DIM // 2, 1)))
        rot = jnp.concatenate(parts, axis=1)
        return zn * cos + rot * sin

    zq = _dot(h, w_ref[:, 0:DA_WIDTH])
    qt_ref[0] = (head_norm_rope(zq, qg_ref[...]) * (QK_SCALE * LOG2E)).T.astype(BF16)
    zk = _dot(h, w_ref[:, DA_WIDTH:2 * DA_WIDTH])
    k_ref[0] = head_norm_rope(zk, kg_ref[...]).astype(BF16)
    vt_ref[0] = _dot(h, w_ref[:, 2 * DA_WIDTH:DA_COLS]).T.astype(BF16)

    i = pl.program_id(1)
    h_ext = jnp.concatenate([h, norm1(xp_ref[0]), norm1(xn_ref[0])], axis=0)
    row = lax.broadcasted_iota(jnp.int32, (tm, 1), 0)
    slab = lambda c0: _dot(h_ext, w_ref[:, DA_COLS + c0:DA_COLS + c0 + SEG_WIDTH])
    z_next_group = slab(0)
    for c0 in range(0, RW_COLS, SEG_WIDTH):
        cols = slice(c0, c0 + SEG_WIDTH)
        z_ext = z_next_group
        if c0 + SEG_WIDTH < RW_COLS:
            z_next_group = slab(c0 + SEG_WIDTH)
        z = z_ext[:tm]
        prev_row = jnp.where(i > 0, z_ext[tm + 7:tm + 8], 0.0)
        next_row = jnp.where(i < pl.num_programs(1) - 1, z_ext[tm + 8:tm + 9], 0.0)
        z_prev = jnp.where(row == 0, prev_row, pltpu.roll(z, 1, 0))
        z_next = jnp.where(row == tm - 1, next_row, pltpu.roll(z, tm - 1, 0))
        z_ref[0, :, cols] = z + mu_ref[0:1, cols] * (z_prev - z) + mu_ref[1:2, cols] * (z_next - z)


def _in_proj(x, g1, w_in, qg, kg, cos_t, sin_t, seg, mu, tm):
    B, S, _ = x.shape
    const = lambda shape: pl.BlockSpec(shape, lambda b, i: (0,) * len(shape))
    tok = lambda width: pl.BlockSpec((1, tm, width), lambda b, i: (b, i, 0))
    tok_t = pl.BlockSpec((1, DA_WIDTH, tm), lambda b, i: (b, 0, i))
    halo_prev = pl.BlockSpec((1, 8, D_MODEL), lambda b, i: (b, jnp.maximum(i * (tm // 8) - 1, 0), 0))
    halo_next = pl.BlockSpec((1, 8, D_MODEL), lambda b, i: (b, jnp.minimum((i + 1) * (tm // 8), S // 8 - 1), 0))
    return pl.pallas_call(
        _in_proj_kernel,
        grid=(B, S // tm),
        in_specs=[tok(D_MODEL), halo_prev, halo_next, const((1, D_MODEL)), const((D_MODEL, IN_COLS)),
                  const((1, DA_WIDTH)), const((1, DA_WIDTH)),
                  pl.BlockSpec((tm, LANES), lambda b, i: (i, 0)),
                  pl.BlockSpec((tm, LANES), lambda b, i: (i, 0)),
                  const((SEG_WIDTH, SEG_WIDTH)), const((2, RW_COLS))],
        out_specs=[tok_t, tok(DA_WIDTH), tok_t, tok(RW_COLS)],
        out_shape=[jax.ShapeDtypeStruct((B, DA_WIDTH, S), BF16), jax.ShapeDtypeStruct((B, S, DA_WIDTH), BF16),
                   jax.ShapeDtypeStruct((B, DA_WIDTH, S), BF16), jax.ShapeDtypeStruct((B, S, RW_COLS), F32)],
        compiler_params=pltpu.CompilerParams(
            dimension_semantics=("parallel", "parallel"), vmem_limit_bytes=VMEM_LIMIT),
        name="in_proj",
    )(x, x, x, g1, w_in, qg, kg, cos_t, sin_t, seg, mu)


def _diff_attn_kernel(qt_ref, qtn_ref, k_ref, vt_ref, lam_ref, sg_ref, o_ref, s_ref, acc_ref, mb_ref, *, tk,
                      per_iter):
    tq = qt_ref.shape[2]
    nk = k_ref.shape[1] // tk
    row = lax.broadcasted_iota(jnp.int32, (LANES, 1), 0)

    def components(qt):
        zero = jnp.zeros_like(qt)
        return jnp.where(row < HEAD_DIM, qt, zero), jnp.where(row >= HEAD_DIM, qt, zero)

    qts = components(qt_ref[0])
    acc_ref[...] = jnp.zeros_like(acc_ref)

    def scores(slot, blk, q_pair):
        kb = k_ref[0, pl.ds(pl.multiple_of(blk * tk, tk), tk), :]
        mblk = []
        for c in range(2):
            s = _dot(kb, q_pair[c])
            s_ref[slot, c] = s
            mblk.append(jnp.max(s, axis=0, keepdims=True))
        return tuple(mblk)

    @pl.when(pl.program_id(2) == 0)
    def _():
        first = scores(0, 0, qts)
        for c in range(2):
            mb_ref[c] = first[c]

    def consume(slot, blk, mblk, ml):
        vtb = vt_ref[0, :, pl.ds(pl.multiple_of(blk * tk, tk), tk)]
        vtb_ones = jnp.concatenate([vtb, jnp.ones((BF16_SUBLANES, tk), BF16)], axis=0)
        out = []
        for c in range(2):
            m, l = ml[c]
            m_new = jnp.maximum(m, mblk[c])
            alpha = jnp.exp2(m - m_new)
            p = jnp.exp2(s_ref[slot, c] - m_new)
            pv = _dot(vtb_ones, p.astype(BF16))
            l = alpha * l + pv[DA_V_DIM:DA_V_DIM + 1]
            acc_ref[c] = alpha * acc_ref[c] + pv[:DA_V_DIM]
            out.append((m_new, l))
        return tuple(out)

    def steps(first, carry, last):
        mblk, ml = carry
        for u in range(per_iter):
            if last and u == per_iter - 1:
                nxt = scores(0, 0, components(qtn_ref[0]))
            else:
                nxt = scores((u + 1) % 2, first + u + 1, qts)
            ml = consume(u % 2, first + u, mblk, ml)
            mblk = nxt
        return mblk, ml

    ml = tuple((jnp.full((1, tq), -1e30, F32), jnp.zeros((1, tq), F32)) for _ in range(2))
    carry = lax.fori_loop(0, nk // per_iter - 1, lambda i, c: steps(i * per_iter, c, False),
                          ((mb_ref[0], mb_ref[1]), ml))
    mb_next, ((_, l0), (_, l1)) = steps(nk - per_iter, carry, True)
    for c in range(2):
        mb_ref[c] = mb_next[c]

    lp = lam_ref[...]
    lam = (jnp.exp(jnp.sum(lp[0:1] * lp[1:2], axis=1, keepdims=True))
           - jnp.exp(jnp.sum(lp[2:3] * lp[3:4], axis=1, keepdims=True)) + LAMBDA_INIT)
    o = (acc_ref[0] / l0 - lam * (acc_ref[1] / l1)).T
    ms = jnp.mean(o * o, axis=-1, keepdims=True)
    o_ref[0] = o * lax.rsqrt(ms + NORM_EPS) * sg_ref[...] * (1.0 - LAMBDA_INIT)


def _diff_attn(qt, k, vt, lam_p, subln_g, tq, tk, per_iter):
    B, S, _ = k.shape
    assert per_iter % 2 == 0 and (S // tk) % per_iter == 0
    nq = S // tq
    return pl.pallas_call(
        functools.partial(_diff_attn_kernel, tk=tk, per_iter=per_iter),
        grid=(B, DA_HEADS, nq),
        in_specs=[pl.BlockSpec((1, LANES, tq), lambda b, h, i: (b, h, i)),
                  pl.BlockSpec((1, LANES, tq), lambda b, h, i: (b, h, jnp.minimum(i + 1, nq - 1))),
                  pl.BlockSpec((1, S, LANES), lambda b, h, i: (b, 0, h)),
                  pl.BlockSpec((1, LANES, S), lambda b, h, i: (b, h, 0)),
                  pl.BlockSpec((4, HEAD_DIM), lambda b, h, i: (0, 0)),
                  pl.BlockSpec((1, DA_V_DIM), lambda b, h, i: (0, 0))],
        out_specs=pl.BlockSpec((1, tq, DA_V_DIM), lambda b, h, i: (b, i, h)),
        out_shape=jax.ShapeDtypeStruct((B, S, DA_WIDTH), F32),
        scratch_shapes=[pltpu.VMEM((2, 2, tk, tq), F32), pltpu.VMEM((2, DA_V_DIM, tq), F32),
                        pltpu.VMEM((2, 1, tq), F32)],
        compiler_params=pltpu.CompilerParams(
            dimension_semantics=("parallel", "parallel", "arbitrary"), vmem_limit_bytes=VMEM_LIMIT),
        name="diff_attn",
    )(qt, qt, k, vt, lam_p, subln_g)


def _rwkv_prep(d, z_ref, pv_ref, wup_ref, aup_ref, gup_ref, seg):
    zs = z_ref[0]

    w = RW_WIDTH
    r, k, v = zs[:, 0:w], zs[:, w:2 * w], zs[:, 2 * w:3 * w]
    wa = zs[:, 3 * w:3 * w + LORA_IN]
    pv = lambda i: pv_ref[i:i + 1]
    w_pre = _dot(jnp.tanh(wa).astype(BF16), wup_ref[d])
    a_pre = _dot(wa.astype(BF16), aup_ref[d])
    lw = -math.exp(-0.5) * jax.nn.sigmoid(pv(PV_W0 + d) + w_pre)
    a_rate = jax.nn.sigmoid(pv(PV_A0 + d) + a_pre)
    kk = k * pv(PV_KK)
    kk = kk * lax.rsqrt(_seg_sum(kk * kk, seg) + KK_EPS)
    kd = k * (1.0 + (a_rate - 1.0) * pv(PV_KA))
    bonus = _seg_sum(r * kd * pv(PV_RK), seg) * v
    out = dict(r=r, v=v, kd=kd, lw=lw, a=-kk, b=kk * a_rate, bonus=bonus)
    if d == 0:
        g_dn = zs[:, 3 * w + LORA_IN:RW_COLS]
        out["g"] = _dot(jax.nn.sigmoid(g_dn).astype(BF16), gup_ref[...])
    return out


def _rwkv_chunks(cis, ops_ref, tri_ref, seg_ref, st_ref, y_refs):
    gw = SEG_WIDTH
    n_groups = RW_HEADS // GROUP_HEADS
    bmask = seg_ref[...]
    ti = lax.broadcasted_iota(jnp.int32, (CHUNK, gw), 0)
    si = lax.broadcasted_iota(jnp.int32, (CHUNK, gw), 1) % HEAD_DIM
    eye = (si == ti).astype(F32)
    strict = ((si < ti), (si > ti))
    incl = ((si <= ti), (si >= ti))
    nt = (((1,), (1,)), ((), ()))
    tn = (((0,), (0,)), ((), ()))
    lane = lax.broadcasted_iota(jnp.int32, (1, LANES), 1)
    half_masks = ((lane < HEAD_DIM).astype(BF16), (lane >= HEAD_DIM).astype(BF16))
    zeros = jnp.zeros((CHUNK, LANES), BF16)

    def blockdiag(x):
        xb = x.astype(BF16)
        blocks = []
        for h in range(GROUP_HEADS):
            part = xb[:, (h // 2) * LANES:(h // 2 + 1) * LANES] * half_masks[h % 2]
            blocks.append(jnp.concatenate([part, zeros] if h < 2 else [zeros, part], axis=1))
        return jnp.concatenate(blocks, axis=0)

    def stack(*xs):
        return jnp.concatenate([x.astype(BF16) for x in xs], axis=0)

    n_seq = len(cis[0])
    rows = [[pl.ds(pl.multiple_of(ci * CHUNK, CHUNK), CHUNK) for ci in cis[d]] for d in range(2)]
    chains = []
    for k in range(n_seq):
        for d in range(2):
            r, v, kd, lw, a, b = (ops_ref[d, n, rows[d][k], :] for n in range(6))
            cl = sum(_dot(tri_ref[d], part) for part in _split3(lw))
            tot = cl[CHUNK - 1:CHUNK] if d == 0 else cl[0:1]
            e_inv = jnp.exp(-cl)
            e_rem = jnp.exp(tot - cl)
            wide = dict(at=a * jnp.exp(cl - lw), rt=r * jnp.exp(cl), bt=b * e_inv, kt=kd * e_inv,
                        bh=b * e_rem, kh=kd * e_rem, v=v, decay=jnp.exp(tot))
            for g in range(n_groups):
                c = {name: x[:, g * gw:(g + 1) * gw] for name, x in wide.items()}
                c.update(k=k, d=d, g=g)
                chains.append(c)

    for c in chains:
        ar = stack(c["at"], c["rt"])
        c["ab"] = lax.dot_general(ar, blockdiag(c["bt"]), nt, preferred_element_type=F32)
        c["ak"] = lax.dot_general(ar, blockdiag(c["kt"]), nt, preferred_element_type=F32)
    for c in chains:
        d = c["d"]
        c["l"] = jnp.where(strict[d], c["ab"][:CHUNK], 0.0)
        c["a_rb"] = jnp.where(incl[d], c["ab"][CHUNK:], 0.0)
        a_ak = jnp.where(strict[d], c["ak"][:CHUNK], 0.0)
        a_rk = jnp.where(incl[d], c["ak"][CHUNK:], 0.0)
        av = _dot(stack(a_ak, a_rk), blockdiag(c["v"]))
        c["akv"], c["arkv"] = av[:CHUNK], av[CHUNK:]

    for c in chains:
        c["t"] = eye + c["l"]
        c["lp"] = _dot(c["l"].astype(BF16), blockdiag(c["l"]))
    for _ in range(int(math.log2(CHUNK)) - 2):
        for c in chains:
            both = _dot(stack(c["t"], c["lp"]), blockdiag(c["lp"]))
            c["t"] = c["t"] + both[:CHUNK]
            c["lp"] = both[CHUNK:]
    for c in chains:
        c["t"] = (c["t"] + _dot(c["t"].astype(BF16), blockdiag(c["lp"]))).astype(BF16)
    for c in chains:
        c["wt"] = _dot(c["t"], blockdiag(c["at"]))
        c["u_loc"] = _dot(c["t"], blockdiag(c["akv"]))

    state = {(d, g): st_ref[d, g] for d in range(2) for g in range(n_groups)}
    for k in range(n_seq):
        now = [c for c in chains if c["k"] == k]
        for c in now:
            wr = _dot(stack(c["wt"], c["rt"]), blockdiag(state[c["d"], c["g"]]))
            c["u"] = wr[:CHUNK] + c["u_loc"]
            c["y"] = wr[CHUNK:] + c["arkv"]
        for c in now:
            c["y"] = c["y"] + _dot(c["a_rb"].astype(BF16), blockdiag(c["u"]))
            st_hi, st_lo = _split2(state[c["d"], c["g"]])
            decay = eye * c["decay"]
            full = lax.dot_general(stack(c["bh"], c["kh"], decay, decay), stack(c["u"], c["v"], st_hi, st_lo),
                                   tn, preferred_element_type=F32) * bmask
            state[c["d"], c["g"]] = sum(full[h * HEAD_DIM:(h + 1) * HEAD_DIM] for h in range(GROUP_HEADS))
        for d in range(2):
            y_refs[d][0, rows[d][k], :] = jnp.concatenate([c["y"] for c in now if c["d"] == d], axis=1)
    for (d, g), st in state.items():
        st_ref[d, g] = st


def _rwkv_kernel(zf_ref, zb_ref, pv_ref, wup_ref, aup_ref, gup_ref, seg_ref, tri_ref,
                 y0_ref, y1_ref, bon0_ref, bon1_ref, g_ref, ops_ref, st_ref):
    tm = zf_ref.shape[1]

    @pl.when(pl.program_id(1) == 0)
    def _():
        st_ref[...] = jnp.zeros_like(st_ref)

    seg = seg_ref[...]
    for d, (z_ref, bon_ref) in enumerate(((zf_ref, bon0_ref), (zb_ref, bon1_ref))):
        p = _rwkv_prep(d, z_ref, pv_ref, wup_ref, aup_ref, gup_ref, seg)
        for n, name in enumerate(("r", "v", "kd", "lw", "a", "b")):
            ops_ref[d, n] = p[name]
        bon_ref[0] = p["bonus"]
        if d == 0:
            g_ref[0] = p["g"]

    n_chunks = tm // CHUNK
    per_iter = min(CHUNKS_PER_ITER, n_chunks)

    def body(it, carry):
        fwd = [it * per_iter + k for k in range(per_iter)]
        bwd = [n_chunks - 1 - c for c in fwd]
        _rwkv_chunks((fwd, bwd), ops_ref, tri_ref, seg_ref, st_ref, (y0_ref, y1_ref))
        return carry

    lax.fori_loop(0, n_chunks // per_iter, body, 0)


def _rwkv_scan(z, pvec, wup, aup, gup, seg, tri, tm):
    B, S, _ = z.shape
    nt = S // tm
    const = lambda shape: pl.BlockSpec(shape, lambda b, i: (0,) * len(shape))
    fwd = lambda width: pl.BlockSpec((1, tm, width), lambda b, i: (b, i, 0))
    bwd = lambda width: pl.BlockSpec((1, tm, width), lambda b, i: (b, nt - 1 - i, 0))
    tok = jax.ShapeDtypeStruct((B, S, RW_WIDTH), F32)
    return pl.pallas_call(
        _rwkv_kernel,
        grid=(B, nt),
        in_specs=[fwd(RW_COLS), bwd(RW_COLS), const((PV_ROWS, RW_WIDTH)),
                  const((2, LORA_IN, RW_WIDTH)), const((2, LORA_IN, RW_WIDTH)), const((GATE_LORA, RW_WIDTH)),
                  const((SEG_WIDTH, SEG_WIDTH)), const((2, CHUNK, CHUNK))],
        out_specs=[fwd(RW_WIDTH), bwd(RW_WIDTH), fwd(RW_WIDTH), bwd(RW_WIDTH), fwd(RW_WIDTH)],
        out_shape=[tok] * 5,
        scratch_shapes=[pltpu.VMEM((2, 6, tm, RW_WIDTH), F32),
                        pltpu.VMEM((2, RW_HEADS // GROUP_HEADS, HEAD_DIM, GROUP_HEADS * HEAD_DIM), F32)],
        compiler_params=pltpu.CompilerParams(
            dimension_semantics=("arbitrary", "arbitrary"), vmem_limit_bytes=VMEM_LIMIT),
        name="rwkv_scan",
    )(z, z, pvec, wup, aup, gup, seg, tri)


def _out_ffn_kernel(x_ref, oda_ref, y0_ref, y1_ref, bon0_ref, bon1_ref, g_ref, pv_ref, seg_ref,
                    wout_ref, g2_ref, w1_ref, w2_ref, o_ref, *, ff_chunk):
    seg = seg_ref[...]
    y = y0_ref[0] + y1_ref[0]
    mean = _seg_sum(y, seg) * (1.0 / HEAD_DIM)
    yc = y - mean
    var = _seg_sum(yc * yc, seg) * (1.0 / HEAD_DIM)
    yn = yc * lax.rsqrt(var + LN_X_EPS) * pv_ref[PV_LNG:PV_LNG + 1] + pv_ref[PV_LNB:PV_LNB + 1]
    o_rw = (yn + bon0_ref[0] + bon1_ref[0]) * g_ref[0]
    x = (x_ref[0] + _dot(oda_ref[0].astype(BF16), wout_ref[0:DA_WIDTH, :])
         + _dot(o_rw.astype(BF16), wout_ref[DA_WIDTH:D_MODEL, :]))
    ms = jnp.mean(x * x, axis=-1, keepdims=True)
    h = (x * lax.rsqrt(ms + NORM_EPS) * g2_ref[...]).astype(BF16)
    ffn = None
    for c in range(D_FF // ff_chunk):
        u = jnp.maximum(_dot(h, w1_ref[:, c * ff_chunk:(c + 1) * ff_chunk]), 0.0)
        part = _dot((u * u).astype(BF16), w2_ref[c * ff_chunk:(c + 1) * ff_chunk, :])
        ffn = part if ffn is None else ffn + part
    o_ref[0] = x + ffn


def _out_ffn(x, o_da, y0, y1, bon0, bon1, g, pvec, seg, w_out, g2, w1, w2, tm):
    B, S, _ = x.shape
    const = lambda shape: pl.BlockSpec(shape, lambda b, i: (0,) * len(shape))
    tok = lambda width: pl.BlockSpec((1, tm, width), lambda b, i: (b, i, 0))
    return pl.pallas_call(
        functools.partial(_out_ffn_kernel, ff_chunk=1024),
        grid=(B, S // tm),
        in_specs=[tok(D_MODEL)] + [tok(RW_WIDTH)] * 6 + [
            const((PV_ROWS, RW_WIDTH)), const((SEG_WIDTH, SEG_WIDTH)), const((D_MODEL, D_MODEL)),
            const((1, D_MODEL)), const((D_MODEL, D_FF)), const((D_FF, D_MODEL))],
        out_specs=tok(D_MODEL),
        out_shape=jax.ShapeDtypeStruct((B, S, D_MODEL), F32),
        compiler_params=pltpu.CompilerParams(
            dimension_semantics=("parallel", "parallel"), vmem_limit_bytes=VMEM_LIMIT),
        name="out_ffn",
    )(x, o_da, y0, y1, bon0, bon1, g, pvec, seg, w_out, g2, w1, w2)


def _rope_tables(seq_len):
    inv_freq = 1.0 / (ROPE_THETA ** (jnp.arange(0, HEAD_DIM, 2, dtype=F32) / HEAD_DIM))
    half = HEAD_DIM // 2
    inv_lanes = jnp.tile(inv_freq, LANES // half)
    sign = jnp.tile(jnp.concatenate([-jnp.ones(half, F32), jnp.ones(half, F32)]), LANES // HEAD_DIM)
    ang = jnp.arange(seq_len, dtype=F32)[:, None] * inv_lanes[None, :]
    return jnp.cos(ang), jnp.sin(ang) * sign


def _attn_tiles(seq_len):
    if seq_len >= 8192:
        return 256, 1024, 4
    tk = _pick(seq_len // 2, 512)
    n_blocks = seq_len // tk
    return _pick(seq_len, 512), tk, n_blocks if n_blocks % 2 == 0 and n_blocks <= 4 else 2


def _pick(n, target):
    t = min(n, target)
    assert n % t == 0, (n, t)
    return t


def kernel(x_prompt, x_sample, norm1_g, w_in, q_norm_g, k_norm_g, lam_q1, lam_k1, lam_q2, lam_k2, subln_g,
           mu_prev, mu_next, w0, w_up, a0, a_up, g_up, k_k, k_a, r_k, ln_x_g, ln_x_b, w_out, norm2_g,
           w_ff1, w_ff2):
    l = 0
    w_in_b = w_in[l].astype(BF16)
    w_out_b = w_out[l].astype(BF16)
    w1_b = w_ff1[l].astype(BF16)
    w2_b = w_ff2[l].astype(BF16)
    g1 = norm1_g[l][None, :]
    g2 = norm2_g[l][None, :]
    qg = jnp.tile(q_norm_g[l], DA_WIDTH // HEAD_DIM)[None, :]
    kg = jnp.tile(k_norm_g[l], DA_WIDTH // HEAD_DIM)[None, :]
    lam_p = jnp.stack([lam_q1[l], lam_k1[l], lam_q2[l], lam_k2[l]])
    sg = subln_g[l][None, :]
    mu = jnp.stack([mu_prev[l], mu_next[l]])
    rows = [w0[l, 0], w0[l, 1], a0[l, 0], a0[l, 1], k_k[l], k_a[l], r_k[l].reshape(-1), ln_x_g[l], ln_x_b[l]]
    pvec = jnp.zeros((PV_ROWS, RW_WIDTH), F32).at[:len(rows)].set(jnp.stack(rows))
    zpad = jnp.zeros((2, DECAY_LORA, RW_WIDTH), F32)
    wup = jnp.concatenate([w_up[l], zpad], axis=1).astype(BF16)
    aup = jnp.concatenate([zpad, a_up[l]], axis=1).astype(BF16)
    gup = g_up[l].astype(BF16)
    ch = jnp.arange(SEG_WIDTH) // HEAD_DIM
    seg = (ch[:, None] == ch[None, :]).astype(BF16)
    t = jnp.arange(CHUNK)
    tri = jnp.stack([t[None, :] <= t[:, None], t[None, :] >= t[:, None]]).astype(BF16)

    cos_t, sin_t = _rope_tables(max(x_prompt.shape[1], x_sample.shape[1]))

    def run(x):
        S = x.shape[1]
        qt, k, vt, z_rw = _in_proj(x, g1, w_in_b, qg, kg, cos_t, sin_t, seg, mu, _pick(S, 512))
        tq, tk, per_iter = _attn_tiles(S)
        o_da = _diff_attn(qt, k, vt, lam_p, sg, tq, tk, per_iter)
        y0, y1, bon0, bon1, g = _rwkv_scan(z_rw, pvec, wup, aup, gup, seg, tri, _pick(S, 256))
        return _out_ffn(x, o_da, y0, y1, bon0, bon1, g, pvec, seg, w_out_b, g2, w1_b, w2_b, _pick(S, 512))

    return (run(x_prompt), run(x_sample))
```

```python
import functools
import math

import jax
import jax.numpy as jnp
from jax import lax
from jax.experimental import pallas as pl
from jax.experimental.pallas import tpu as pltpu

F32 = jnp.float32
BF16 = jnp.bfloat16

D_MODEL = 1024
DA_HEADS = 4
HEAD_DIM = 64
DA_V_DIM = 128
DA_WIDTH = DA_HEADS * DA_V_DIM
RW_WIDTH = D_MODEL - DA_WIDTH
RW_HEADS = RW_WIDTH // HEAD_DIM
DECAY_LORA = 64
ICLR_LORA = 64
GATE_LORA = 128
LORA_IN = DECAY_LORA + ICLR_LORA
RW_COLS = 3 * RW_WIDTH + LORA_IN + GATE_LORA
DA_COLS = 3 * DA_WIDTH
IN_COLS = DA_COLS + RW_COLS
D_FF = 4 * D_MODEL
ROPE_THETA = 10000.0
NORM_EPS = 1e-6
LN_X_EPS = 64e-5
KK_EPS = 1e-12
LAMBDA_INIT = 0.8 - 0.6 * math.exp(-0.3 * 0)
LOG2E = 1.4426950408889634
QK_SCALE = HEAD_DIM ** -0.5

LANES = 128
BF16_SUBLANES = 16
CHUNK = 64
GROUP_HEADS = 4
SEG_WIDTH = GROUP_HEADS * HEAD_DIM
CHUNKS_PER_ITER = 4
VMEM_LIMIT = 56 * 1024 * 1024

PV_W0, PV_A0, PV_KK, PV_KA, PV_RK, PV_LNG, PV_LNB = 0, 2, 4, 5, 6, 7, 8
PV_ROWS = 16


def _dot(a, b):
    return jnp.dot(a, b, preferred_element_type=F32)


def _split2(x):
    hi = x.astype(BF16)
    lo = (x - hi.astype(F32)).astype(BF16)
    return hi, lo


def _split3(x):
    hi = x.astype(BF16)
    r1 = x - hi.astype(F32)
    mid = r1.astype(BF16)
    lo = (r1 - mid.astype(F32)).astype(BF16)
    return hi, mid, lo


def _seg_sum(x, seg):
    xb = x.astype(BF16)
    gw = seg.shape[0]
    return jnp.concatenate([_dot(xb[:, g * gw:(g + 1) * gw], seg) for g in range(x.shape[1] // gw)], axis=1)


def _in_proj_kernel(x_ref, xp_ref, xn_ref, g1_ref, w_ref, qg_ref, kg_ref, cos_ref, sin_ref, seg_ref, mu_ref,
                    qt_ref, k_ref, vt_ref, z_ref):
    def norm1(x):
        ms = jnp.mean(x * x, axis=-1, keepdims=True)
        return (x * lax.rsqrt(ms + NORM_EPS) * g1_ref[...]).astype(BF16)

    h = norm1(x_ref[0])
    tm = h.shape[0]
    seg = seg_ref[...]
    reps = DA_WIDTH // LANES
    cos = jnp.concatenate([cos_ref[...]] * reps, axis=1)
    sin = jnp.concatenate([sin_ref[...]] * reps, axis=1)
    lane = lax.broadcasted_iota(jnp.int32, (1, LANES), 1)
    first_half = (lane % HEAD_DIM) < (HEAD_DIM // 2)

    def head_norm_rope(z, g):
        ss = _seg_sum(z * z, seg) * (1.0 / HEAD_DIM)
        zn = z * lax.rsqrt(ss + NORM_EPS) * g
        parts = []
        for c in range(reps):
            zc = zn[:, c * LANES:(c + 1) * LANES]
            parts.append(jnp.where(first_half,
                                   pltpu.roll(zc, LANES - HEAD_DIM // 2, 1),
                                   pltpu.roll(zc, HEAD_DIM // 2, 1)))
        rot = jnp.concatenate(parts, axis=1)
        return zn * cos + rot * sin

    zq = _dot(h, w_ref[:, 0:DA_WIDTH])
    qt_ref[0] = (head_norm_rope(zq, qg_ref[...]) * (QK_SCALE * LOG2E)).T.astype(BF16)
    zk = _dot(h, w_ref[:, DA_WIDTH:2 * DA_WIDTH])
    k_ref[0] = head_norm_rope(zk, kg_ref[...]).astype(BF16)
    vt_ref[0] = _dot(h, w_ref[:, 2 * DA_WIDTH:DA_COLS]).T.astype(BF16)

    i = pl.program_id(1)
    h_ext = jnp.concatenate([h, norm1(xp_ref[0]), norm1(xn_ref[0])], axis=0)
    row = lax.broadcasted_iota(jnp.int32, (tm, 1), 0)
    slab = lambda c0: _dot(h_ext, w_ref[:, DA_COLS + c0:DA_COLS + c0 + SEG_WIDTH])
    z_next_group = slab(0)
    for c0 in range(0, RW_COLS, SEG_WIDTH):
        cols = slice(c0, c0 + SEG_WIDTH)
        z_ext = z_next_group
        if c0 + SEG_WIDTH < RW_COLS:
            z_next_group = slab(c0 + SEG_WIDTH)
        z = z_ext[:tm]
        prev_row = jnp.where(i > 0, z_ext[tm + 7:tm + 8], 0.0)
        next_row = jnp.where(i < pl.num_programs(1) - 1, z_ext[tm + 8:tm + 9], 0.0)
        z_prev = jnp.where(row == 0, prev_row, pltpu.roll(z, 1, 0))
        z_next = jnp.where(row == tm - 1, next_row, pltpu.roll(z, tm - 1, 0))
        z_ref[0, :, cols] = z + mu_ref[0:1, cols] * (z_prev - z) + mu_ref[1:2, cols] * (z_next - z)


def _in_proj(x, g1, w_in, qg, kg, cos_t, sin_t, seg, mu, tm):
    B, S, _ = x.shape
    const = lambda shape: pl.BlockSpec(shape, lambda b, i: (0,) * len(shape))
    tok = lambda width: pl.BlockSpec((1, tm, width), lambda b, i: (b, i, 0))
    tok_t = pl.BlockSpec((1, DA_WIDTH, tm), lambda b, i: (b, 0, i))
    halo_prev = pl.BlockSpec((1, 8, D_MODEL), lambda b, i: (b, jnp.maximum(i * (tm // 8) - 1, 0), 0))
    halo_next = pl.BlockSpec((1, 8, D_MODEL), lambda b, i: (b, jnp.minimum((i + 1) * (tm // 8), S // 8 - 1), 0))
    return pl.pallas_call(
        _in_proj_kernel,
        grid=(B, S // tm),
        in_specs=[tok(D_MODEL), halo_prev, halo_next, const((1, D_MODEL)), const((D_MODEL, IN_COLS)),
                  const((1, DA_WIDTH)), const((1, DA_WIDTH)),
                  pl.BlockSpec((tm, LANES), lambda b, i: (i, 0)),
                  pl.BlockSpec((tm, LANES), lambda b, i: (i, 0)),
                  const((SEG_WIDTH, SEG_WIDTH)), const((2, RW_COLS))],
        out_specs=[tok_t, tok(DA_WIDTH), tok_t, tok(RW_COLS)],
        out_shape=[jax.ShapeDtypeStruct((B, DA_WIDTH, S), BF16), jax.ShapeDtypeStruct((B, S, DA_WIDTH), BF16),
                   jax.ShapeDtypeStruct((B, DA_WIDTH, S), BF16), jax.ShapeDtypeStruct((B, S, RW_COLS), F32)],
        compiler_params=pltpu.CompilerParams(
            dimension_semantics=("parallel", "parallel"), vmem_limit_bytes=VMEM_LIMIT),
        name="in_proj",
    )(x, x, x, g1, w_in, qg, kg, cos_t, sin_t, seg, mu)


def _diff_attn_kernel(qt_ref, qtn_ref, k_ref, vt_ref, lam_ref, sg_ref, o_ref, s_ref, acc_ref, mb_ref, *, tk,
                      per_iter):
    tq = qt_ref.shape[2]
    nk = k_ref.shape[1] // tk
    row = lax.broadcasted_iota(jnp.int32, (LANES, 1), 0)

    def components(qt):
        zero = jnp.zeros_like(qt)
        return jnp.where(row < HEAD_DIM, qt, zero), jnp.where(row >= HEAD_DIM, qt, zero)

    qts = components(qt_ref[0])
    acc_ref[...] = jnp.zeros_like(acc_ref)

    def scores(slot, blk, q_pair):
        kb = k_ref[0, pl.ds(pl.multiple_of(blk * tk, tk), tk), :]
        mblk = []
        for c in range(2):
            s = _dot(kb, q_pair[c])
            s_ref[slot, c] = s
            mblk.append(jnp.max(s, axis=0, keepdims=True))
        return tuple(mblk)

    @pl.when(pl.program_id(2) == 0)
    def _():
        first = scores(0, 0, qts)
        for c in range(2):
            mb_ref[c] = first[c]

    def consume(slot, blk, mblk, ml):
        vtb = vt_ref[0, :, pl.ds(pl.multiple_of(blk * tk, tk), tk)]
        vtb_ones = jnp.concatenate([vtb, jnp.ones((BF16_SUBLANES, tk), BF16)], axis=0)
        out = []
        for c in range(2):
            m, l = ml[c]
            m_new = jnp.maximum(m, mblk[c])
            alpha = jnp.exp2(m - m_new)
            p = jnp.exp2(s_ref[slot, c] - m_new)
            pv = _dot(vtb_ones, p.astype(BF16))
            l = alpha * l + pv[DA_V_DIM:DA_V_DIM + 1]
            acc_ref[c] = alpha * acc_ref[c] + pv[:DA_V_DIM]
            out.append((m_new, l))
        return tuple(out)

    def steps(first, carry, last):
        mblk, ml = carry
        for u in range(per_iter):
            if last and u == per_iter - 1:
                nxt = scores(0, 0, components(qtn_ref[0]))
            else:
                nxt = scores((u + 1) % 2, first + u + 1, qts)
            ml = consume(u % 2, first + u, mblk, ml)
            mblk = nxt
        return mblk, ml

    ml = tuple((jnp.full((1, tq), -1e30, F32), jnp.zeros((1, tq), F32)) for _ in range(2))
    carry = lax.fori_loop(0, nk // per_iter - 1, lambda i, c: steps(i * per_iter, c, False),
                          ((mb_ref[0], mb_ref[1]), ml))
    mb_next, ((_, l0), (_, l1)) = steps(nk - per_iter, carry, True)
    for c in range(2):
        mb_ref[c] = mb_next[c]

    lp = lam_ref[...]
    lam = (jnp.exp(jnp.sum(lp[0:1] * lp[1:2], axis=1, keepdims=True))
           - jnp.exp(jnp.sum(lp[2:3] * lp[3:4], axis=1, keepdims=True)) + LAMBDA_INIT)
    o = (acc_ref[0] / l0 - lam * (acc_ref[1] / l1)).T
    ms = jnp.mean(o * o, axis=-1, keepdims=True)
    o_ref[0] = o * lax.rsqrt(ms + NORM_EPS) * sg_ref[...] * (1.0 - LAMBDA_INIT)


def _diff_attn(qt, k, vt, lam_p, subln_g, tq, tk, per_iter):
    B, S, _ = k.shape
    assert per_iter % 2 == 0 and (S // tk) % per_iter == 0
    nq = S // tq
    return pl.pallas_call(
        functools.partial(_diff_attn_kernel, tk=tk, per_iter=per_iter),
        grid=(B, DA_HEADS, nq),
        in_specs=[pl.BlockSpec((1, LANES, tq), lambda b, h, i: (b, h, i)),
                  pl.BlockSpec((1, LANES, tq), lambda b, h, i: (b, h, jnp.minimum(i + 1, nq - 1))),
                  pl.BlockSpec((1, S, LANES), lambda b, h, i: (b, 0, h)),
                  pl.BlockSpec((1, LANES, S), lambda b, h, i: (b, h, 0)),
                  pl.BlockSpec((4, HEAD_DIM), lambda b, h, i: (0, 0)),
                  pl.BlockSpec((1, DA_V_DIM), lambda b, h, i: (0, 0))],
        out_specs=pl.BlockSpec((1, tq, DA_V_DIM), lambda b, h, i: (b, i, h)),
        out_shape=jax.ShapeDtypeStruct((B, S, DA_WIDTH), F32),
        scratch_shapes=[pltpu.VMEM((2, 2, tk, tq), F32), pltpu.VMEM((2, DA_V_DIM, tq), F32),
                        pltpu.VMEM((2, 1, tq), F32)],
        compiler_params=pltpu.CompilerParams(
            dimension_semantics=("parallel", "parallel", "arbitrary"), vmem_limit_bytes=VMEM_LIMIT),
        name="diff_attn",
    )(qt, qt, k, vt, lam_p, subln_g)


def _rwkv_prep(d, z_ref, pv_ref, wup_ref, aup_ref, gup_ref, seg):
    zs = z_ref[0]

    w = RW_WIDTH
    r, k, v = zs[:, 0:w], zs[:, w:2 * w], zs[:, 2 * w:3 * w]
    wa = zs[:, 3 * w:3 * w + LORA_IN]
    pv = lambda i: pv_ref[i:i + 1]
    w_pre = _dot(jnp.tanh(wa).astype(BF16), wup_ref[d])
    a_pre = _dot(wa.astype(BF16), aup_ref[d])
    lw = -math.exp(-0.5) * jax.nn.sigmoid(pv(PV_W0 + d) + w_pre)
    a_rate = jax.nn.sigmoid(pv(PV_A0 + d) + a_pre)
    kk = k * pv(PV_KK)
    kk = kk * lax.rsqrt(_seg_sum(kk * kk, seg) + KK_EPS)
    kd = k * (1.0 + (a_rate - 1.0) * pv(PV_KA))
    bonus = _seg_sum(r * kd * pv(PV_RK), seg) * v
    out = dict(r=r, v=v, kd=kd, lw=lw, a=-kk, b=kk * a_rate, bonus=bonus)
    if d == 0:
        g_dn = zs[:, 3 * w + LORA_IN:RW_COLS]
        out["g"] = _dot(jax.nn.sigmoid(g_dn).astype(BF16), gup_ref[...])
    return out


def _rwkv_chunks(cis, ops_ref, tri_ref, seg_ref, st_ref, y_refs):
    gw = SEG_WIDTH
    n_groups = RW_HEADS // GROUP_HEADS
    bmask = seg_ref[...]
    ti = lax.broadcasted_iota(jnp.int32, (CHUNK, gw), 0)
    si = lax.broadcasted_iota(jnp.int32, (CHUNK, gw), 1) % HEAD_DIM
    eye = (si == ti).astype(F32)
    strict = ((si < ti), (si > ti))
    incl = ((si <= ti), (si >= ti))
    nt = (((1,), (1,)), ((), ()))
    tn = (((0,), (0,)), ((), ()))
    lane = lax.broadcasted_iota(jnp.int32, (1, LANES), 1)
    half_masks = ((lane < HEAD_DIM).astype(BF16), (lane >= HEAD_DIM).astype(BF16))
    zeros = jnp.zeros((CHUNK, LANES), BF16)

    def blockdiag(x):
        xb = x.astype(BF16)
        blocks = []
        for h in range(GROUP_HEADS):
            part = xb[:, (h // 2) * LANES:(h // 2 + 1) * LANES] * half_masks[h % 2]
            blocks.append(jnp.concatenate([part, zeros] if h < 2 else [zeros, part], axis=1))
        return jnp.concatenate(blocks, axis=0)

    def stack(*xs):
        return jnp.concatenate([x.astype(BF16) for x in xs], axis=0)

    n_seq = len(cis[0])
    rows = [[pl.ds(pl.multiple_of(ci * CHUNK, CHUNK), CHUNK) for ci in cis[d]] for d in range(2)]
    chains = []
    for k in range(n_seq):
        for d in range(2):
            r, v, kd, lw, a, b = (ops_ref[d, n, rows[d][k], :] for n in range(6))
            cl = sum(_dot(tri_ref[d], part) for part in _split3(lw))
            tot = cl[CHUNK - 1:CHUNK] if d == 0 else cl[0:1]
            e_inv = jnp.exp(-cl)
            e_rem = jnp.exp(tot - cl)
            wide = dict(at=a * jnp.exp(cl - lw), rt=r * jnp.exp(cl), bt=b * e_inv, kt=kd * e_inv,
                        bh=b * e_rem, kh=kd * e_rem, v=v, decay=jnp.exp(tot))
            for g in range(n_groups):
                c = {name: x[:, g * gw:(g + 1) * gw] for name, x in wide.items()}
                c.update(k=k, d=d, g=g)
                chains.append(c)

    for c in chains:
        ar = stack(c["at"], c["rt"])
        c["ab"] = lax.dot_general(ar, blockdiag(c["bt"]), nt, preferred_element_type=F32)
        c["ak"] = lax.dot_general(ar, blockdiag(c["kt"]), nt, preferred_element_type=F32)
    for c in chains:
        d = c["d"]
        c["l"] = jnp.where(strict[d], c["ab"][:CHUNK], 0.0)
        c["a_rb"] = jnp.where(incl[d], c["ab"][CHUNK:], 0.0)
        a_ak = jnp.where(strict[d], c["ak"][:CHUNK], 0.0)
        a_rk = jnp.where(incl[d], c["ak"][CHUNK:], 0.0)
        av = _dot(stack(a_ak, a_rk), blockdiag(c["v"]))
        c["akv"], c["arkv"] = av[:CHUNK], av[CHUNK:]

    for c in chains:
        c["t"] = eye + c["l"]
        c["lp"] = _dot(c["l"].astype(BF16), blockdiag(c["l"]))
    for _ in range(int(math.log2(CHUNK)) - 2):
        for c in chains:
            both = _dot(stack(c["t"], c["lp"]), blockdiag(c["lp"]))
            c["t"] = c["t"] + both[:CHUNK]
            c["lp"] = both[CHUNK:]
    for c in chains:
        c["t"] = (c["t"] + _dot(c["t"].astype(BF16), blockdiag(c["lp"]))).astype(BF16)
    for c in chains:
        c["wt"] = _dot(c["t"], blockdiag(c["at"]))
        c["u_loc"] = _dot(c["t"], blockdiag(c["akv"]))

    state = {(d, g): st_ref[d, g] for d in range(2) for g in range(n_groups)}
    for k in range(n_seq):
        now = [c for c in chains if c["k"] == k]
        for c in now:
            wr = _dot(stack(c["wt"], c["rt"]), blockdiag(state[c["d"], c["g"]]))
            c["u"] = wr[:CHUNK] + c["u_loc"]
            c["y"] = wr[CHUNK:] + c["arkv"]
        for c in now:
            c["y"] = c["y"] + _dot(c["a_rb"].astype(BF16), blockdiag(c["u"]))
            st_hi, st_lo = _split2(state[c["d"], c["g"]])
            decay = eye * c["decay"]
            full = lax.dot_general(stack(c["bh"], c["kh"], decay, decay), stack(c["u"], c["v"], st_hi, st_lo),
                                   tn, preferred_element_type=F32) * bmask
            state[c["d"], c["g"]] = sum(full[h * HEAD_DIM:(h + 1) * HEAD_DIM] for h in range(GROUP_HEADS))
        for d in range(2):
            y_refs[d][0, rows[d][k], :] = jnp.concatenate([c["y"] for c in now if c["d"] == d], axis=1)
    for (d, g), st in state.items():
        st_ref[d, g] = st


def _rwkv_kernel(zf_ref, zb_ref, pv_ref, wup_ref, aup_ref, gup_ref, seg_ref, tri_ref,
                 y0_ref, y1_ref, bon0_ref, bon1_ref, g_ref, ops_ref, st_ref):
    tm = zf_ref.shape[1]

    @pl.when(pl.program_id(1) == 0)
    def _():
        st_ref[...] = jnp.zeros_like(st_ref)

    seg = seg_ref[...]
    for d, (z_ref, bon_ref) in enumerate(((zf_ref, bon0_ref), (zb_ref, bon1_ref))):
        p = _rwkv_prep(d, z_ref, pv_ref, wup_ref, aup_ref, gup_ref, seg)
        for n, name in enumerate(("r", "v", "kd", "lw", "a", "b")):
            ops_ref[d, n] = p[name]
        bon_ref[0] = p["bonus"]
        if d == 0:
            g_ref[0] = p["g"]

    n_chunks = tm // CHUNK
    per_iter = min(CHUNKS_PER_ITER, n_chunks)

    def body(it, carry):
        fwd = [it * per_iter + k for k in range(per_iter)]
        bwd = [n_chunks - 1 - c for c in fwd]
        _rwkv_chunks((fwd, bwd), ops_ref, tri_ref, seg_ref, st_ref, (y0_ref, y1_ref))
        return carry

    lax.fori_loop(0, n_chunks // per_iter, body, 0)


def _rwkv_scan(z, pvec, wup, aup, gup, seg, tri, tm):
    B, S, _ = z.shape
    nt = S // tm
    const = lambda shape: pl.BlockSpec(shape, lambda b, i: (0,) * len(shape))
    fwd = lambda width: pl.BlockSpec((1, tm, width), lambda b, i: (b, i, 0))
    bwd = lambda width: pl.BlockSpec((1, tm, width), lambda b, i: (b, nt - 1 - i, 0))
    tok = jax.ShapeDtypeStruct((B, S, RW_WIDTH), F32)
    return pl.pallas_call(
        _rwkv_kernel,
        grid=(B, nt),
        in_specs=[fwd(RW_COLS), bwd(RW_COLS), const((PV_ROWS, RW_WIDTH)),
                  const((2, LORA_IN, RW_WIDTH)), const((2, LORA_IN, RW_WIDTH)), const((GATE_LORA, RW_WIDTH)),
                  const((SEG_WIDTH, SEG_WIDTH)), const((2, CHUNK, CHUNK))],
        out_specs=[fwd(RW_WIDTH), bwd(RW_WIDTH), fwd(RW_WIDTH), bwd(RW_WIDTH), fwd(RW_WIDTH)],
        out_shape=[tok] * 5,
        scratch_shapes=[pltpu.VMEM((2, 6, tm, RW_WIDTH), F32),
                        pltpu.VMEM((2, RW_HEADS // GROUP_HEADS, HEAD_DIM, GROUP_HEADS * HEAD_DIM), F32)],
        compiler_params=pltpu.CompilerParams(
            dimension_semantics=("arbitrary", "arbitrary"), vmem_limit_bytes=VMEM_LIMIT),
        name="rwkv_scan",
    )(z, z, pvec, wup, aup, gup, seg, tri)


def _out_ffn_kernel(x_ref, oda_ref, y0_ref, y1_ref, bon0_ref, bon1_ref, g_ref, pv_ref, seg_ref,
                    wout_ref, g2_ref, w1_ref, w2_ref, o_ref, *, ff_chunk):
    seg = seg_ref[...]
    y = y0_ref[0] + y1_ref[0]
    mean = _seg_sum(y, seg) * (1.0 / HEAD_DIM)
    yc = y - mean
    var = _seg_sum(yc * yc, seg) * (1.0 / HEAD_DIM)
    yn = yc * lax.rsqrt(var + LN_X_EPS) * pv_ref[PV_LNG:PV_LNG + 1] + pv_ref[PV_LNB:PV_LNB + 1]
    o_rw = (yn + bon0_ref[0] + bon1_ref[0]) * g_ref[0]
    x = (x_ref[0] + _dot(oda_ref[0].astype(BF16), wout_ref[0:DA_WIDTH, :])
         + _dot(o_rw.astype(BF16), wout_ref[DA_WIDTH:D_MODEL, :]))
    ms = jnp.mean(x * x, axis=-1, keepdims=True)
    h = (x * lax.rsqrt(ms + NORM_EPS) * g2_ref[...]).astype(BF16)
    ffn = None
    for c in range(D_FF // ff_chunk):
        u = jnp.maximum(_dot(h, w1_ref[:, c * ff_chunk:(c + 1) * ff_chunk]), 0.0)
        part = _dot((u * u).astype(BF16), w2_ref[c * ff_chunk:(c + 1) * ff_chunk, :])
        ffn = part if ffn is None else ffn + part
    o_ref[0] = x + ffn


def _out_ffn(x, o_da, y0, y1, bon0, bon1, g, pvec, seg, w_out, g2, w1, w2, tm):
    B, S, _ = x.shape
    const = lambda shape: pl.BlockSpec(shape, lambda b, i: (0,) * len(shape))
    tok = lambda width: pl.BlockSpec((1, tm, width), lambda b, i: (b, i, 0))
    return pl.pallas_call(
        functools.partial(_out_ffn_kernel, ff_chunk=1024),
        grid=(B, S // tm),
        in_specs=[tok(D_MODEL)] + [tok(RW_WIDTH)] * 6 + [
            const((PV_ROWS, RW_WIDTH)), const((SEG_WIDTH, SEG_WIDTH)), const((D_MODEL, D_MODEL)),
            const((1, D_MODEL)), const((D_MODEL, D_FF)), const((D_FF, D_MODEL))],
        out_specs=tok(D_MODEL),
        out_shape=jax.ShapeDtypeStruct((B, S, D_MODEL), F32),
        compiler_params=pltpu.CompilerParams(
            dimension_semantics=("parallel", "parallel"), vmem_limit_bytes=VMEM_LIMIT),
        name="out_ffn",
    )(x, o_da, y0, y1, bon0, bon1, g, pvec, seg, w_out, g2, w1, w2)


def _rope_tables(seq_len):
    inv_freq = 1.0 / (ROPE_THETA ** (jnp.arange(0, HEAD_DIM, 2, dtype=F32) / HEAD_DIM))
    half = HEAD_DIM // 2
    inv_lanes = jnp.tile(inv_freq, LANES // half)
    sign = jnp.tile(jnp.concatenate([-jnp.ones(half, F32), jnp.ones(half, F32)]), LANES // HEAD_DIM)
    ang = jnp.arange(seq_len, dtype=F32)[:, None] * inv_lanes[None, :]
    return jnp.cos(ang), jnp.sin(ang) * sign


def _attn_tiles(seq_len):
    if seq_len >= 8192:
        tq, tk = 256, 1024
    else:
        tq, tk = _pick(seq_len, 512), _pick(seq_len // 2, 512)
    n_blocks = seq_len // tk
    return tq, tk, n_blocks if n_blocks % 2 == 0 and n_blocks <= 16 else 2


def _pick(n, target):
    t = min(n, target)
    assert n % t == 0, (n, t)
    return t


def kernel(x_prompt, x_sample, norm1_g, w_in, q_norm_g, k_norm_g, lam_q1, lam_k1, lam_q2, lam_k2, subln_g,
           mu_prev, mu_next, w0, w_up, a0, a_up, g_up, k_k, k_a, r_k, ln_x_g, ln_x_b, w_out, norm2_g,
           w_ff1, w_ff2):
    l = 0
    w_in_b = w_in[l].astype(BF16)
    w_out_b = w_out[l].astype(BF16)
    w1_b = w_ff1[l].astype(BF16)
    w2_b = w_ff2[l].astype(BF16)
    g1 = norm1_g[l][None, :]
    g2 = norm2_g[l][None, :]
    qg = jnp.tile(q_norm_g[l], DA_WIDTH // HEAD_DIM)[None, :]
    kg = jnp.tile(k_norm_g[l], DA_WIDTH // HEAD_DIM)[None, :]
    lam_p = jnp.stack([lam_q1[l], lam_k1[l], lam_q2[l], lam_k2[l]])
    sg = subln_g[l][None, :]
    mu = jnp.stack([mu_prev[l], mu_next[l]])
    rows = [w0[l, 0], w0[l, 1], a0[l, 0], a0[l, 1], k_k[l], k_a[l], r_k[l].reshape(-1), ln_x_g[l], ln_x_b[l]]
    pvec = jnp.zeros((PV_ROWS, RW_WIDTH), F32).at[:len(rows)].set(jnp.stack(rows))
    zpad = jnp.zeros((2, DECAY_LORA, RW_WIDTH), F32)
    wup = jnp.concatenate([w_up[l], zpad], axis=1).astype(BF16)
    aup = jnp.concatenate([zpad, a_up[l]], axis=1).astype(BF16)
    gup = g_up[l].astype(BF16)
    ch = jnp.arange(SEG_WIDTH) // HEAD_DIM
    seg = (ch[:, None] == ch[None, :]).astype(BF16)
    t = jnp.arange(CHUNK)
    tri = jnp.stack([t[None, :] <= t[:, None], t[None, :] >= t[:, None]]).astype(BF16)

    cos_t, sin_t = _rope_tables(max(x_prompt.shape[1], x_sample.shape[1]))

    def run(x):
        S = x.shape[1]
        qt, k, vt, z_rw = _in_proj(x, g1, w_in_b, qg, kg, cos_t, sin_t, seg, mu, _pick(S, 512))
        tq, tk, per_iter = _attn_tiles(S)
        o_da = _diff_attn(qt, k, vt, lam_p, sg, tq, tk, per_iter)
        y0, y1, bon0, bon1, g = _rwkv_scan(z_rw, pvec, wup, aup, gup, seg, tri, _pick(S, 256))
        return _out_ffn(x, o_da, y0, y1, bon0, bon1, g, pvec, seg, w_out_b, g2, w1_b, w2_b, _pick(S, 512))

    return (run(x_prompt), run(x_sample))
```

```python
import functools
import math

import jax
import jax.numpy as jnp
from jax import lax
from jax.experimental import pallas as pl
from jax.experimental.pallas import tpu as pltpu

F32 = jnp.float32
BF16 = jnp.bfloat16

D_MODEL = 1024
DA_HEADS = 4
HEAD_DIM = 64
DA_V_DIM = 128
DA_WIDTH = DA_HEADS * DA_V_DIM
RW_WIDTH = D_MODEL - DA_WIDTH
RW_HEADS = RW_WIDTH // HEAD_DIM
DECAY_LORA = 64
ICLR_LORA = 64
GATE_LORA = 128
LORA_IN = DECAY_LORA + ICLR_LORA
RW_COLS = 3 * RW_WIDTH + LORA_IN + GATE_LORA
DA_COLS = 3 * DA_WIDTH
IN_COLS = DA_COLS + RW_COLS
D_FF = 4 * D_MODEL
ROPE_THETA = 10000.0
NORM_EPS = 1e-6
LN_X_EPS = 64e-5
KK_EPS = 1e-12
LAMBDA_INIT = 0.8 - 0.6 * math.exp(-0.3 * 0)
LOG2E = 1.4426950408889634
QK_SCALE = HEAD_DIM ** -0.5

LANES = 128
BF16_SUBLANES = 16
CHUNK = 64
GROUP_HEADS = 4
SEG_WIDTH = GROUP_HEADS * HEAD_DIM
CHUNKS_PER_ITER = 4
SCAN_TILE = CHUNK * CHUNKS_PER_ITER
TOKEN_TILE = 512
FF_CHUNK = 1024
VMEM_LIMIT = 56 * 1024 * 1024

PV_W0, PV_A0, PV_KK, PV_KA, PV_RK, PV_LNG, PV_LNB = 0, 2, 4, 5, 6, 7, 8
PV_ROWS = 16


def _dot(a, b):
    return jnp.dot(a, b, preferred_element_type=F32)


def _split2(x):
    hi = x.astype(BF16)
    lo = (x - hi.astype(F32)).astype(BF16)
    return hi, lo


def _seg_sum(x, seg):
    xb = x.astype(BF16)
    gw = seg.shape[0]
    return jnp.concatenate([_dot(xb[:, g * gw:(g + 1) * gw], seg) for g in range(x.shape[1] // gw)], axis=1)


def _in_proj_kernel(x_ref, xp_ref, xn_ref, g1_ref, w_ref, qg_ref, kg_ref, cos_ref, sin_ref, seg_ref, mu_ref,
                    qt_ref, k_ref, vt_ref, z_ref):
    def norm1(x):
        ms = jnp.mean(x * x, axis=-1, keepdims=True)
        return (x * lax.rsqrt(ms + NORM_EPS) * g1_ref[...]).astype(BF16)

    h = norm1(x_ref[0])
    tm = h.shape[0]
    seg = seg_ref[...]
    reps = DA_WIDTH // LANES
    cos = jnp.concatenate([cos_ref[...]] * reps, axis=1)
    sin = jnp.concatenate([sin_ref[...]] * reps, axis=1)
    lane = lax.broadcasted_iota(jnp.int32, (1, LANES), 1)
    first_half = (lane % HEAD_DIM) < (HEAD_DIM // 2)

    def head_norm_rope(z, g):
        ss = _seg_sum(z * z, seg) * (1.0 / HEAD_DIM)
        zn = z * lax.rsqrt(ss + NORM_EPS) * g
        parts = []
        for c in range(reps):
            zc = zn[:, c * LANES:(c + 1) * LANES]
            parts.append(jnp.where(first_half,
                                   pltpu.roll(zc, LANES - HEAD_DIM // 2, 1),
                                   pltpu.roll(zc, HEAD_DIM // 2, 1)))
        rot = jnp.concatenate(parts, axis=1)
        return zn * cos + rot * sin

    zq = _dot(h, w_ref[:, 0:DA_WIDTH])
    qt_ref[0] = (head_norm_rope(zq, qg_ref[...]) * (QK_SCALE * LOG2E)).T.astype(BF16)
    zk = _dot(h, w_ref[:, DA_WIDTH:2 * DA_WIDTH])
    k_ref[0] = head_norm_rope(zk, kg_ref[...]).astype(BF16)
    vt_ref[0] = _dot(h, w_ref[:, 2 * DA_WIDTH:DA_COLS]).T.astype(BF16)

    i = pl.program_id(1)
    h_ext = jnp.concatenate([h, norm1(xp_ref[0]), norm1(xn_ref[0])], axis=0)
    row = lax.broadcasted_iota(jnp.int32, (tm, 1), 0)
    slab = lambda c0: _dot(h_ext, w_ref[:, DA_COLS + c0:DA_COLS + c0 + SEG_WIDTH])
    z_next_group = slab(0)
    for c0 in range(0, RW_COLS, SEG_WIDTH):
        cols = slice(c0, c0 + SEG_WIDTH)
        z_ext = z_next_group
        if c0 + SEG_WIDTH < RW_COLS:
            z_next_group = slab(c0 + SEG_WIDTH)
        z = z_ext[:tm]
        prev_row = jnp.where(i > 0, z_ext[tm + 7:tm + 8], 0.0)
        next_row = jnp.where(i < pl.num_programs(1) - 1, z_ext[tm + 8:tm + 9], 0.0)
        z_prev = jnp.where(row == 0, prev_row, pltpu.roll(z, 1, 0))
        z_next = jnp.where(row == tm - 1, next_row, pltpu.roll(z, tm - 1, 0))
        z_ref[0, :, cols] = z + mu_ref[0:1, cols] * (z_prev - z) + mu_ref[1:2, cols] * (z_next - z)


def _in_proj(x, g1, w_in, qg, kg, cos_t, sin_t, seg, mu, tm):
    B, S, _ = x.shape
    const = lambda shape: pl.BlockSpec(shape, lambda b, i: (0,) * len(shape))
    tok = lambda width: pl.BlockSpec((1, tm, width), lambda b, i: (b, i, 0))
    tok_t = pl.BlockSpec((1, DA_WIDTH, tm), lambda b, i: (b, 0, i))
    halo_prev = pl.BlockSpec((1, 8, D_MODEL), lambda b, i: (b, jnp.maximum(i * (tm // 8) - 1, 0), 0))
    halo_next = pl.BlockSpec((1, 8, D_MODEL), lambda b, i: (b, jnp.minimum((i + 1) * (tm // 8), S // 8 - 1), 0))
    return pl.pallas_call(
        _in_proj_kernel,
        grid=(B, S // tm),
        in_specs=[tok(D_MODEL), halo_prev, halo_next, const((1, D_MODEL)), const((D_MODEL, IN_COLS)),
                  const((1, DA_WIDTH)), const((1, DA_WIDTH)),
                  pl.BlockSpec((tm, LANES), lambda b, i: (i, 0)),
                  pl.BlockSpec((tm, LANES), lambda b, i: (i, 0)),
                  const((SEG_WIDTH, SEG_WIDTH)), const((2, RW_COLS))],
        out_specs=[tok_t, tok(DA_WIDTH), tok_t, tok(RW_COLS)],
        out_shape=[jax.ShapeDtypeStruct((B, DA_WIDTH, S), BF16), jax.ShapeDtypeStruct((B, S, DA_WIDTH), BF16),
                   jax.ShapeDtypeStruct((B, DA_WIDTH, S), BF16), jax.ShapeDtypeStruct((B, S, RW_COLS), F32)],
        compiler_params=pltpu.CompilerParams(
            dimension_semantics=("parallel", "parallel"), vmem_limit_bytes=VMEM_LIMIT),
        name="in_proj",
    )(x, x, x, g1, w_in, qg, kg, cos_t, sin_t, seg, mu)


def _diff_attn_kernel(qt_ref, qtn_ref, k_ref, vt_ref, lam_ref, sg_ref, o_ref, s_ref, acc_ref, mb_ref, *, tk,
                      per_iter):
    tq = qt_ref.shape[2]
    nk = k_ref.shape[1] // tk
    row = lax.broadcasted_iota(jnp.int32, (LANES, 1), 0)

    def components(qt):
        zero = jnp.zeros_like(qt)
        return jnp.where(row < HEAD_DIM, qt, zero), jnp.where(row >= HEAD_DIM, qt, zero)

    qts = components(qt_ref[0])
    acc_ref[...] = jnp.zeros_like(acc_ref)

    def scores(slot, blk, q_pair):
        kb = k_ref[0, pl.ds(pl.multiple_of(blk * tk, tk), tk), :]
        mblk = []
        for c in range(2):
            s = _dot(kb, q_pair[c])
            s_ref[slot, c] = s
            mblk.append(jnp.max(s, axis=0, keepdims=True))
        return tuple(mblk)

    @pl.when(pl.program_id(2) == 0)
    def _():
        first = scores(0, 0, qts)
        for c in range(2):
            mb_ref[c] = first[c]

    def consume(slot, blk, mblk, ml):
        vtb = vt_ref[0, :, pl.ds(pl.multiple_of(blk * tk, tk), tk)]
        vtb_ones = jnp.concatenate([vtb, jnp.ones((BF16_SUBLANES, tk), BF16)], axis=0)
        out = []
        for c in range(2):
            m, l = ml[c]
            m_new = jnp.maximum(m, mblk[c])
            alpha = jnp.exp2(m - m_new)
            p = jnp.exp2(s_ref[slot, c] - m_new)
            pv = _dot(vtb_ones, p.astype(BF16))
            l = alpha * l + pv[DA_V_DIM:DA_V_DIM + 1]
            acc_ref[c] = alpha * acc_ref[c] + pv[:DA_V_DIM]
            out.append((m_new, l))
        return tuple(out)

    def steps(first, carry, last):
        mblk, ml = carry
        for u in range(per_iter):
            if last and u == per_iter - 1:
                nxt = scores(0, 0, components(qtn_ref[0]))
            else:
                nxt = scores((u + 1) % 2, first + u + 1, qts)
            ml = consume(u % 2, first + u, mblk, ml)
            mblk = nxt
        return mblk, ml

    ml = tuple((jnp.full((1, tq), -1e30, F32), jnp.zeros((1, tq), F32)) for _ in range(2))
    carry = lax.fori_loop(0, nk // per_iter - 1, lambda i, c: steps(i * per_iter, c, False),
                          ((mb_ref[0], mb_ref[1]), ml))
    mb_next, ((_, l0), (_, l1)) = steps(nk - per_iter, carry, True)
    for c in range(2):
        mb_ref[c] = mb_next[c]

    lp = lam_ref[...]
    lam = (jnp.exp(jnp.sum(lp[0:1] * lp[1:2], axis=1, keepdims=True))
           - jnp.exp(jnp.sum(lp[2:3] * lp[3:4], axis=1, keepdims=True)) + LAMBDA_INIT)
    o = (acc_ref[0] / l0 - lam * (acc_ref[1] / l1)).T
    ms = jnp.mean(o * o, axis=-1, keepdims=True)
    o_ref[0] = o * lax.rsqrt(ms + NORM_EPS) * sg_ref[...] * (1.0 - LAMBDA_INIT)


def _diff_attn(qt, k, vt, lam_p, subln_g, tq, tk, per_iter):
    B, S, _ = k.shape
    assert per_iter % 2 == 0 and (S // tk) % per_iter == 0
    nq = S // tq
    return pl.pallas_call(
        functools.partial(_diff_attn_kernel, tk=tk, per_iter=per_iter),
        grid=(B, DA_HEADS, nq),
        in_specs=[pl.BlockSpec((1, LANES, tq), lambda b, h, i: (b, h, i)),
                  pl.BlockSpec((1, LANES, tq), lambda b, h, i: (b, h, jnp.minimum(i + 1, nq - 1))),
                  pl.BlockSpec((1, S, LANES), lambda b, h, i: (b, 0, h)),
                  pl.BlockSpec((1, LANES, S), lambda b, h, i: (b, h, 0)),
                  pl.BlockSpec((4, HEAD_DIM), lambda b, h, i: (0, 0)),
                  pl.BlockSpec((1, DA_V_DIM), lambda b, h, i: (0, 0))],
        out_specs=pl.BlockSpec((1, tq, DA_V_DIM), lambda b, h, i: (b, i, h)),
        out_shape=jax.ShapeDtypeStruct((B, S, DA_WIDTH), F32),
        scratch_shapes=[pltpu.VMEM((2, 2, tk, tq), F32), pltpu.VMEM((2, DA_V_DIM, tq), F32),
                        pltpu.VMEM((2, 1, tq), F32)],
        compiler_params=pltpu.CompilerParams(
            dimension_semantics=("parallel", "parallel", "arbitrary"), vmem_limit_bytes=VMEM_LIMIT),
        name="diff_attn",
    )(qt, qt, k, vt, lam_p, subln_g)


def _rwkv_prep(d, z_ref, pv_ref, wup_ref, aup_ref, gup_ref, seg):
    zs = z_ref[0]

    w = RW_WIDTH
    r, k, v = zs[:, 0:w], zs[:, w:2 * w], zs[:, 2 * w:3 * w]
    wa = zs[:, 3 * w:3 * w + LORA_IN]
    pv = lambda i: pv_ref[i:i + 1]
    w_pre = _dot(jnp.tanh(wa).astype(BF16), wup_ref[d])
    a_pre = _dot(wa.astype(BF16), aup_ref[d])
    lw = -math.exp(-0.5) * jax.nn.sigmoid(pv(PV_W0 + d) + w_pre)
    a_rate = jax.nn.sigmoid(pv(PV_A0 + d) + a_pre)
    kk = k * pv(PV_KK)
    kk = kk * lax.rsqrt(_seg_sum(kk * kk, seg) + KK_EPS)
    kd = k * (1.0 + (a_rate - 1.0) * pv(PV_KA))
    bonus = _seg_sum(r * kd * pv(PV_RK), seg) * v
    out = dict(r=r, v=v, kd=kd, lw=lw, a=-kk, b=kk * a_rate, bonus=bonus)
    if d == 0:
        g_dn = zs[:, 3 * w + LORA_IN:RW_COLS]
        out["g"] = _dot(jax.nn.sigmoid(g_dn).astype(BF16), gup_ref[...])
    return out


def _rwkv_chunks(cis, ops_ref, tri_ref, seg_ref, st_ref, y_refs):
    gw = SEG_WIDTH
    n_groups = RW_HEADS // GROUP_HEADS
    bmask = seg_ref[...]
    ti = lax.broadcasted_iota(jnp.int32, (CHUNK, gw), 0)
    si = lax.broadcasted_iota(jnp.int32, (CHUNK, gw), 1) % HEAD_DIM
    eye = (si == ti).astype(F32)
    strict = ((si < ti), (si > ti))
    incl = ((si <= ti), (si >= ti))
    nt = (((1,), (1,)), ((), ()))
    tn = (((0,), (0,)), ((), ()))
    lane = lax.broadcasted_iota(jnp.int32, (1, LANES), 1)
    half_masks = ((lane < HEAD_DIM).astype(BF16), (lane >= HEAD_DIM).astype(BF16))
    zeros = jnp.zeros((CHUNK, LANES), BF16)

    def blockdiag(x):
        xb = x.astype(BF16)
        blocks = []
        for h in range(GROUP_HEADS):
            part = xb[:, (h // 2) * LANES:(h // 2 + 1) * LANES] * half_masks[h % 2]
            blocks.append(jnp.concatenate([part, zeros] if h < 2 else [zeros, part], axis=1))
        return jnp.concatenate(blocks, axis=0)

    def stack(*xs):
        return jnp.concatenate([x.astype(BF16) for x in xs], axis=0)

    n_seq = len(cis[0])
    rows = [[pl.ds(pl.multiple_of(ci * CHUNK, CHUNK), CHUNK) for ci in cis[d]] for d in range(2)]
    chains = []
    for k in range(n_seq):
        for d in range(2):
            r, v, kd, lw, a, b = (ops_ref[d, n, rows[d][k], :] for n in range(6))
            cl = sum(_dot(tri_ref[d], part) for part in _split2(lw))
            tot = cl[CHUNK - 1:CHUNK] if d == 0 else cl[0:1]
            e_inv = jnp.exp(-cl)
            e_rem = jnp.exp(tot - cl)
            wide = dict(at=a * jnp.exp(cl - lw), rt=r * jnp.exp(cl), bt=b * e_inv, kt=kd * e_inv,
                        bh=b * e_rem, kh=kd * e_rem, v=v)
            wide = {name: x.astype(BF16) for name, x in wide.items()}
            wide["decay"] = jnp.exp(tot)
            for g in range(n_groups):
                c = {name: x[:, g * gw:(g + 1) * gw] for name, x in wide.items()}
                c.update(k=k, d=d, g=g)
                chains.append(c)

    for c in chains:
        ar = stack(c["at"], c["rt"])
        c["ab"] = lax.dot_general(ar, blockdiag(c["bt"]), nt, preferred_element_type=F32)
        c["ak"] = lax.dot_general(ar, blockdiag(c["kt"]), nt, preferred_element_type=F32)
    for c in chains:
        d = c["d"]
        c["l"] = jnp.where(strict[d], c["ab"][:CHUNK], 0.0)
        c["a_rb"] = jnp.where(incl[d], c["ab"][CHUNK:], 0.0).astype(BF16)
        a_ak = jnp.where(strict[d], c["ak"][:CHUNK], 0.0)
        a_rk = jnp.where(incl[d], c["ak"][CHUNK:], 0.0)
        av = _dot(stack(a_ak, a_rk), blockdiag(c["v"]))
        c["akv"], c["arkv"] = av[:CHUNK].astype(BF16), av[CHUNK:]

    for c in chains:
        c["t"] = eye + c["l"]
        c["lp"] = _dot(c["l"].astype(BF16), blockdiag(c["l"]))
    for _ in range(int(math.log2(CHUNK)) - 2):
        for c in chains:
            both = _dot(stack(c["t"], c["lp"]), blockdiag(c["lp"]))
            c["t"] = c["t"] + both[:CHUNK]
            c["lp"] = both[CHUNK:]
    for c in chains:
        c["t"] = (c["t"] + _dot(c["t"].astype(BF16), blockdiag(c["lp"]))).astype(BF16)
    for c in chains:
        c["wt"] = _dot(c["t"], blockdiag(c["at"])).astype(BF16)
        c["u_loc"] = _dot(c["t"], blockdiag(c["akv"]))

    state = {(d, g): st_ref[d, g] for d in range(2) for g in range(n_groups)}
    for k in range(n_seq):
        now = [c for c in chains if c["k"] == k]
        for c in now:
            wr = _dot(stack(c["wt"], c["rt"]), blockdiag(state[c["d"], c["g"]]))
            c["u"] = wr[:CHUNK] + c["u_loc"]
            c["y"] = wr[CHUNK:] + c["arkv"]
        for c in now:
            c["y"] = c["y"] + _dot(c["a_rb"], blockdiag(c["u"]))
            st_hi, st_lo = _split2(state[c["d"], c["g"]])
            decay = eye * c["decay"]
            full = lax.dot_general(stack(c["bh"], c["kh"], decay, decay), stack(c["u"], c["v"], st_hi, st_lo),
                                   tn, preferred_element_type=F32) * bmask
            state[c["d"], c["g"]] = sum(full[h * HEAD_DIM:(h + 1) * HEAD_DIM] for h in range(GROUP_HEADS))
        for d in range(2):
            y_refs[d][0, rows[d][k], :] = jnp.concatenate([c["y"] for c in now if c["d"] == d], axis=1)
    for (d, g), st in state.items():
        st_ref[d, g] = st


def _rwkv_kernel(zf_ref, zb_ref, pv_ref, wup_ref, aup_ref, gup_ref, seg_ref, tri_ref,
                 y0_ref, y1_ref, bon0_ref, bon1_ref, g_ref, ops_ref, st_ref):
    tm = zf_ref.shape[1]

    @pl.when(pl.program_id(1) == 0)
    def _():
        st_ref[...] = jnp.zeros_like(st_ref)

    seg = seg_ref[...]
    for d, (z_ref, bon_ref) in enumerate(((zf_ref, bon0_ref), (zb_ref, bon1_ref))):
        p = _rwkv_prep(d, z_ref, pv_ref, wup_ref, aup_ref, gup_ref, seg)
        for n, name in enumerate(("r", "v", "kd", "lw", "a", "b")):
            ops_ref[d, n] = p[name]
        bon_ref[0] = p["bonus"]
        if d == 0:
            g_ref[0] = p["g"]

    n_chunks = tm // CHUNK
    per_iter = min(CHUNKS_PER_ITER, n_chunks)

    def body(it, carry):
        fwd = [it * per_iter + k for k in range(per_iter)]
        bwd = [n_chunks - 1 - c for c in fwd]
        _rwkv_chunks((fwd, bwd), ops_ref, tri_ref, seg_ref, st_ref, (y0_ref, y1_ref))
        return carry

    lax.fori_loop(0, n_chunks // per_iter, body, 0)


def _rwkv_scan(z, pvec, wup, aup, gup, seg, tri, tm):
    B, S, _ = z.shape
    nt = S // tm
    const = lambda shape: pl.BlockSpec(shape, lambda b, i: (0,) * len(shape))
    fwd = lambda width: pl.BlockSpec((1, tm, width), lambda b, i: (b, i, 0))
    bwd = lambda width: pl.BlockSpec((1, tm, width), lambda b, i: (b, nt - 1 - i, 0))
    tok = jax.ShapeDtypeStruct((B, S, RW_WIDTH), F32)
    return pl.pallas_call(
        _rwkv_kernel,
        grid=(B, nt),
        in_specs=[fwd(RW_COLS), bwd(RW_COLS), const((PV_ROWS, RW_WIDTH)),
                  const((2, LORA_IN, RW_WIDTH)), const((2, LORA_IN, RW_WIDTH)), const((GATE_LORA, RW_WIDTH)),
                  const((SEG_WIDTH, SEG_WIDTH)), const((2, CHUNK, CHUNK))],
        out_specs=[fwd(RW_WIDTH), bwd(RW_WIDTH), fwd(RW_WIDTH), bwd(RW_WIDTH), fwd(RW_WIDTH)],
        out_shape=[tok] * 5,
        scratch_shapes=[pltpu.VMEM((2, 6, tm, RW_WIDTH), F32),
                        pltpu.VMEM((2, RW_HEADS // GROUP_HEADS, HEAD_DIM, GROUP_HEADS * HEAD_DIM), F32)],
        compiler_params=pltpu.CompilerParams(
            dimension_semantics=("arbitrary", "arbitrary"), vmem_limit_bytes=VMEM_LIMIT),
        name="rwkv_scan",
    )(z, z, pvec, wup, aup, gup, seg, tri)


def _out_ffn_kernel(x_ref, oda_ref, y0_ref, y1_ref, bon0_ref, bon1_ref, g_ref, pv_ref, seg_ref,
                    wout_ref, g2_ref, w1_ref, w2_ref, o_ref):
    seg = seg_ref[...]
    y = y0_ref[0] + y1_ref[0]
    mean = _seg_sum(y, seg) * (1.0 / HEAD_DIM)
    yc = y - mean
    var = _seg_sum(yc * yc, seg) * (1.0 / HEAD_DIM)
    yn = yc * lax.rsqrt(var + LN_X_EPS) * pv_ref[PV_LNG:PV_LNG + 1] + pv_ref[PV_LNB:PV_LNB + 1]
    o_rw = (yn + bon0_ref[0] + bon1_ref[0]) * g_ref[0]
    x = (x_ref[0] + _dot(oda_ref[0].astype(BF16), wout_ref[0:DA_WIDTH, :])
         + _dot(o_rw.astype(BF16), wout_ref[DA_WIDTH:D_MODEL, :]))
    ms = jnp.mean(x * x, axis=-1, keepdims=True)
    h = (x * lax.rsqrt(ms + NORM_EPS) * g2_ref[...]).astype(BF16)
    ffn = None
    for c0 in range(0, D_FF, FF_CHUNK):
        u = jnp.maximum(_dot(h, w1_ref[:, c0:c0 + FF_CHUNK]), 0.0)
        part = _dot((u * u).astype(BF16), w2_ref[c0:c0 + FF_CHUNK, :])
        ffn = part if ffn is None else ffn + part
    o_ref[0] = x + ffn


def _out_ffn(x, o_da, y0, y1, bon0, bon1, g, pvec, seg, w_out, g2, w1, w2, tm):
    B, S, _ = x.shape
    const = lambda shape: pl.BlockSpec(shape, lambda b, i: (0,) * len(shape))
    tok = lambda width: pl.BlockSpec((1, tm, width), lambda b, i: (b, i, 0))
    return pl.pallas_call(
        _out_ffn_kernel,
        grid=(B, S // tm),
        in_specs=[tok(D_MODEL)] + [tok(RW_WIDTH)] * 6 + [
            const((PV_ROWS, RW_WIDTH)), const((SEG_WIDTH, SEG_WIDTH)), const((D_MODEL, D_MODEL)),
            const((1, D_MODEL)), const((D_MODEL, D_FF)), const((D_FF, D_MODEL))],
        out_specs=tok(D_MODEL),
        out_shape=jax.ShapeDtypeStruct((B, S, D_MODEL), F32),
        compiler_params=pltpu.CompilerParams(
            dimension_semantics=("parallel", "parallel"), vmem_limit_bytes=VMEM_LIMIT),
        name="out_ffn",
    )(x, o_da, y0, y1, bon0, bon1, g, pvec, seg, w_out, g2, w1, w2)


def _rope_tables(seq_len):
    inv_freq = 1.0 / (ROPE_THETA ** (jnp.arange(0, HEAD_DIM, 2, dtype=F32) / HEAD_DIM))
    half = HEAD_DIM // 2
    inv_lanes = jnp.tile(inv_freq, LANES // half)
    sign = jnp.tile(jnp.concatenate([-jnp.ones(half, F32), jnp.ones(half, F32)]), LANES // HEAD_DIM)
    ang = jnp.arange(seq_len, dtype=F32)[:, None] * inv_lanes[None, :]
    return jnp.cos(ang), jnp.sin(ang) * sign


def _attn_tiles(seq_len):
    if seq_len >= 8192:
        tq, tk = 256, 1024
    else:
        tq, tk = _pick(seq_len, 512), _pick(seq_len // 2, 512)
    n_blocks = seq_len // tk
    return tq, tk, n_blocks if n_blocks % 2 == 0 and n_blocks <= 16 else 2


def _pick(n, target):
    t = min(n, target)
    assert n % t == 0, (n, t)
    return t


def kernel(x_prompt, x_sample, norm1_g, w_in, q_norm_g, k_norm_g, lam_q1, lam_k1, lam_q2, lam_k2, subln_g,
           mu_prev, mu_next, w0, w_up, a0, a_up, g_up, k_k, k_a, r_k, ln_x_g, ln_x_b, w_out, norm2_g,
           w_ff1, w_ff2):
    l = 0
    w_in_b = w_in[l].astype(BF16)
    w_out_b = w_out[l].astype(BF16)
    w1_b = w_ff1[l].astype(BF16)
    w2_b = w_ff2[l].astype(BF16)
    g1 = norm1_g[l][None, :]
    g2 = norm2_g[l][None, :]
    qg = jnp.tile(q_norm_g[l], DA_WIDTH // HEAD_DIM)[None, :]
    kg = jnp.tile(k_norm_g[l], DA_WIDTH // HEAD_DIM)[None, :]
    lam_p = jnp.stack([lam_q1[l], lam_k1[l], lam_q2[l], lam_k2[l]])
    sg = subln_g[l][None, :]
    mu = jnp.stack([mu_prev[l], mu_next[l]])
    rows = [w0[l, 0], w0[l, 1], a0[l, 0], a0[l, 1], k_k[l], k_a[l], r_k[l].reshape(-1), ln_x_g[l], ln_x_b[l]]
    pvec = jnp.zeros((PV_ROWS, RW_WIDTH), F32).at[:len(rows)].set(jnp.stack(rows))
    zpad = jnp.zeros((2, DECAY_LORA, RW_WIDTH), F32)
    wup = jnp.concatenate([w_up[l], zpad], axis=1).astype(BF16)
    aup = jnp.concatenate([zpad, a_up[l]], axis=1).astype(BF16)
    gup = g_up[l].astype(BF16)
    ch = jnp.arange(SEG_WIDTH) // HEAD_DIM
    seg = (ch[:, None] == ch[None, :]).astype(BF16)
    t = jnp.arange(CHUNK)
    tri = jnp.stack([t[None, :] <= t[:, None], t[None, :] >= t[:, None]]).astype(BF16)

    cos_t, sin_t = _rope_tables(max(x_prompt.shape[1], x_sample.shape[1]))

    def run(x):
        S = x.shape[1]
        tm = _pick(S, TOKEN_TILE)
        qt, k, vt, z_rw = _in_proj(x, g1, w_in_b, qg, kg, cos_t, sin_t, seg, mu, tm)
        tq, tk, per_iter = _attn_tiles(S)
        o_da = _diff_attn(qt, k, vt, lam_p, sg, tq, tk, per_iter)
        y0, y1, bon0, bon1, g = _rwkv_scan(z_rw, pvec, wup, aup, gup, seg, tri, _pick(S, SCAN_TILE))
        return _out_ffn(x, o_da, y0, y1, bon0, bon1, g, pvec, seg, w_out_b, g2, w1_b, w2_b, tm)

    return (run(x_prompt), run(x_sample))
```

```python
import functools
import math

import jax
import jax.numpy as jnp
from jax import lax
from jax.experimental import pallas as pl
from jax.experimental.pallas import tpu as pltpu

F32 = jnp.float32
BF16 = jnp.bfloat16

D_MODEL = 1024
DA_HEADS = 4
HEAD_DIM = 64
DA_V_DIM = 128
DA_WIDTH = DA_HEADS * DA_V_DIM
RW_WIDTH = D_MODEL - DA_WIDTH
RW_HEADS = RW_WIDTH // HEAD_DIM
DECAY_LORA = 64
ICLR_LORA = 64
GATE_LORA = 128
LORA_IN = DECAY_LORA + ICLR_LORA
RW_COLS = 3 * RW_WIDTH + LORA_IN + GATE_LORA
DA_COLS = 3 * DA_WIDTH
IN_COLS = DA_COLS + RW_COLS
D_FF = 4 * D_MODEL
ROPE_THETA = 10000.0
NORM_EPS = 1e-6
LN_X_EPS = 64e-5
KK_EPS = 1e-12
LAMBDA_INIT = 0.8 - 0.6 * math.exp(-0.3 * 0)
LOG2E = 1.4426950408889634
QK_SCALE = HEAD_DIM ** -0.5

LANES = 128
BF16_SUBLANES = 16
CHUNK = 64
GROUP_HEADS = 4
SEG_WIDTH = GROUP_HEADS * HEAD_DIM
CHUNKS_PER_ITER = 4
CHUNK_SKEW = 2
SCAN_TILE = CHUNK * CHUNKS_PER_ITER
TOKEN_TILE = 512
FF_CHUNK = 1024
VMEM_LIMIT = 56 * 1024 * 1024

PV_W0, PV_A0, PV_KK, PV_KA, PV_RK, PV_LNG, PV_LNB = 0, 2, 4, 5, 6, 7, 8
PV_ROWS = 16


def _dot(a, b):
    return jnp.dot(a, b, preferred_element_type=F32)


def _split2(x):
    hi = x.astype(BF16)
    lo = (x - hi.astype(F32)).astype(BF16)
    return hi, lo


def _seg_sum(x, seg):
    xb = x.astype(BF16)
    gw = seg.shape[0]
    return jnp.concatenate([_dot(xb[:, g * gw:(g + 1) * gw], seg) for g in range(x.shape[1] // gw)], axis=1)


def _in_proj_kernel(x_ref, xp_ref, xn_ref, g1_ref, w_ref, qg_ref, kg_ref, cos_ref, sin_ref, seg_ref, mu_ref,
                    qt_ref, k_ref, vt_ref, z_ref):
    def norm1(x):
        ms = jnp.mean(x * x, axis=-1, keepdims=True)
        return (x * lax.rsqrt(ms + NORM_EPS) * g1_ref[...]).astype(BF16)

    h = norm1(x_ref[0])
    tm = h.shape[0]
    seg = seg_ref[...]
    reps = DA_WIDTH // LANES
    cos = jnp.concatenate([cos_ref[...]] * reps, axis=1)
    sin = jnp.concatenate([sin_ref[...]] * reps, axis=1)
    lane = lax.broadcasted_iota(jnp.int32, (1, LANES), 1)
    first_half = (lane % HEAD_DIM) < (HEAD_DIM // 2)

    def head_norm_rope(z, g):
        ss = _seg_sum(z * z, seg) * (1.0 / HEAD_DIM)
        zn = z * lax.rsqrt(ss + NORM_EPS) * g
        parts = []
        for c in range(reps):
            zc = zn[:, c * LANES:(c + 1) * LANES]
            parts.append(jnp.where(first_half,
                                   pltpu.roll(zc, LANES - HEAD_DIM // 2, 1),
                                   pltpu.roll(zc, HEAD_DIM // 2, 1)))
        rot = jnp.concatenate(parts, axis=1)
        return zn * cos + rot * sin

    zq = _dot(h, w_ref[:, 0:DA_WIDTH])
    qt_ref[0] = (head_norm_rope(zq, qg_ref[...]) * (QK_SCALE * LOG2E)).T.astype(BF16)
    zk = _dot(h, w_ref[:, DA_WIDTH:2 * DA_WIDTH])
    k_ref[0] = head_norm_rope(zk, kg_ref[...]).astype(BF16)
    vt_ref[0] = _dot(h, w_ref[:, 2 * DA_WIDTH:DA_COLS]).T.astype(BF16)

    i = pl.program_id(1)
    h_ext = jnp.concatenate([h, norm1(xp_ref[0]), norm1(xn_ref[0])], axis=0)
    row = lax.broadcasted_iota(jnp.int32, (tm, 1), 0)
    slab = lambda c0: _dot(h_ext, w_ref[:, DA_COLS + c0:DA_COLS + c0 + SEG_WIDTH])
    z_next_group = slab(0)
    for c0 in range(0, RW_COLS, SEG_WIDTH):
        cols = slice(c0, c0 + SEG_WIDTH)
        z_ext = z_next_group
        if c0 + SEG_WIDTH < RW_COLS:
            z_next_group = slab(c0 + SEG_WIDTH)
        z = z_ext[:tm]
        prev_row = jnp.where(i > 0, z_ext[tm + 7:tm + 8], 0.0)
        next_row = jnp.where(i < pl.num_programs(1) - 1, z_ext[tm + 8:tm + 9], 0.0)
        z_prev = jnp.where(row == 0, prev_row, pltpu.roll(z, 1, 0))
        z_next = jnp.where(row == tm - 1, next_row, pltpu.roll(z, tm - 1, 0))
        z_ref[0, :, cols] = z + mu_ref[0:1, cols] * (z_prev - z) + mu_ref[1:2, cols] * (z_next - z)


def _in_proj(x, g1, w_in, qg, kg, cos_t, sin_t, seg, mu, tm):
    B, S, _ = x.shape
    const = lambda shape: pl.BlockSpec(shape, lambda b, i: (0,) * len(shape))
    tok = lambda width: pl.BlockSpec((1, tm, width), lambda b, i: (b, i, 0))
    tok_t = pl.BlockSpec((1, DA_WIDTH, tm), lambda b, i: (b, 0, i))
    halo_prev = pl.BlockSpec((1, 8, D_MODEL), lambda b, i: (b, jnp.maximum(i * (tm // 8) - 1, 0), 0))
    halo_next = pl.BlockSpec((1, 8, D_MODEL), lambda b, i: (b, jnp.minimum((i + 1) * (tm // 8), S // 8 - 1), 0))
    return pl.pallas_call(
        _in_proj_kernel,
        grid=(B, S // tm),
        in_specs=[tok(D_MODEL), halo_prev, halo_next, const((1, D_MODEL)), const((D_MODEL, IN_COLS)),
                  const((1, DA_WIDTH)), const((1, DA_WIDTH)),
                  pl.BlockSpec((tm, LANES), lambda b, i: (i, 0)),
                  pl.BlockSpec((tm, LANES), lambda b, i: (i, 0)),
                  const((SEG_WIDTH, SEG_WIDTH)), const((2, RW_COLS))],
        out_specs=[tok_t, tok(DA_WIDTH), tok_t, tok(RW_COLS)],
        out_shape=[jax.ShapeDtypeStruct((B, DA_WIDTH, S), BF16), jax.ShapeDtypeStruct((B, S, DA_WIDTH), BF16),
                   jax.ShapeDtypeStruct((B, DA_WIDTH, S), BF16), jax.ShapeDtypeStruct((B, S, RW_COLS), F32)],
        compiler_params=pltpu.CompilerParams(
            dimension_semantics=("parallel", "parallel"), vmem_limit_bytes=VMEM_LIMIT),
        name="in_proj",
    )(x, x, x, g1, w_in, qg, kg, cos_t, sin_t, seg, mu)


def _diff_attn_kernel(qt_ref, qtn_ref, k_ref, vt_ref, lam_ref, sg_ref, o_ref, s_ref, acc_ref, mb_ref, *, tk,
                      per_iter):
    tq = qt_ref.shape[2]
    nk = k_ref.shape[1] // tk
    row = lax.broadcasted_iota(jnp.int32, (LANES, 1), 0)

    def components(qt):
        zero = jnp.zeros_like(qt)
        return jnp.where(row < HEAD_DIM, qt, zero), jnp.where(row >= HEAD_DIM, qt, zero)

    qts = components(qt_ref[0])
    acc_ref[...] = jnp.zeros_like(acc_ref)

    def scores(slot, blk, q_pair):
        kb = k_ref[0, pl.ds(pl.multiple_of(blk * tk, tk), tk), :]
        mblk = []
        for c in range(2):
            s = _dot(kb, q_pair[c])
            s_ref[slot, c] = s
            mblk.append(jnp.max(s, axis=0, keepdims=True))
        return tuple(mblk)

    @pl.when(pl.program_id(2) == 0)
    def _():
        first = scores(0, 0, qts)
        for c in range(2):
            mb_ref[c] = first[c]

    def consume(slot, blk, mblk, ml):
        vtb = vt_ref[0, :, pl.ds(pl.multiple_of(blk * tk, tk), tk)]
        vtb_ones = jnp.concatenate([vtb, jnp.ones((BF16_SUBLANES, tk), BF16)], axis=0)
        out = []
        for c in range(2):
            m, l = ml[c]
            m_new = jnp.maximum(m, mblk[c])
            alpha = jnp.exp2(m - m_new)
            p = jnp.exp2(s_ref[slot, c] - m_new)
            pv = _dot(vtb_ones, p.astype(BF16))
            l = alpha * l + pv[DA_V_DIM:DA_V_DIM + 1]
            acc_ref[c] = alpha * acc_ref[c] + pv[:DA_V_DIM]
            out.append((m_new, l))
        return tuple(out)

    def steps(first, carry, last):
        mblk, ml = carry
        for u in range(per_iter):
            if last and u == per_iter - 1:
                nxt = scores(0, 0, components(qtn_ref[0]))
            else:
                nxt = scores((u + 1) % 2, first + u + 1, qts)
            ml = consume(u % 2, first + u, mblk, ml)
            mblk = nxt
        return mblk, ml

    ml = tuple((jnp.full((1, tq), -1e30, F32), jnp.zeros((1, tq), F32)) for _ in range(2))
    carry = lax.fori_loop(0, nk // per_iter - 1, lambda i, c: steps(i * per_iter, c, False),
                          ((mb_ref[0], mb_ref[1]), ml))
    mb_next, ((_, l0), (_, l1)) = steps(nk - per_iter, carry, True)
    for c in range(2):
        mb_ref[c] = mb_next[c]

    lp = lam_ref[...]
    lam = (jnp.exp(jnp.sum(lp[0:1] * lp[1:2], axis=1, keepdims=True))
           - jnp.exp(jnp.sum(lp[2:3] * lp[3:4], axis=1, keepdims=True)) + LAMBDA_INIT)
    o = (acc_ref[0] / l0 - lam * (acc_ref[1] / l1)).T
    ms = jnp.mean(o * o, axis=-1, keepdims=True)
    o_ref[0] = o * lax.rsqrt(ms + NORM_EPS) * sg_ref[...] * (1.0 - LAMBDA_INIT)


def _diff_attn(qt, k, vt, lam_p, subln_g, tq, tk, per_iter):
    B, S, _ = k.shape
    assert per_iter % 2 == 0 and (S // tk) % per_iter == 0
    nq = S // tq
    return pl.pallas_call(
        functools.partial(_diff_attn_kernel, tk=tk, per_iter=per_iter),
        grid=(B, DA_HEADS, nq),
        in_specs=[pl.BlockSpec((1, LANES, tq), lambda b, h, i: (b, h, i)),
                  pl.BlockSpec((1, LANES, tq), lambda b, h, i: (b, h, jnp.minimum(i + 1, nq - 1))),
                  pl.BlockSpec((1, S, LANES), lambda b, h, i: (b, 0, h)),
                  pl.BlockSpec((1, LANES, S), lambda b, h, i: (b, h, 0)),
                  pl.BlockSpec((4, HEAD_DIM), lambda b, h, i: (0, 0)),
                  pl.BlockSpec((1, DA_V_DIM), lambda b, h, i: (0, 0))],
        out_specs=pl.BlockSpec((1, tq, DA_V_DIM), lambda b, h, i: (b, i, h)),
        out_shape=jax.ShapeDtypeStruct((B, S, DA_WIDTH), F32),
        scratch_shapes=[pltpu.VMEM((2, 2, tk, tq), F32), pltpu.VMEM((2, DA_V_DIM, tq), F32),
                        pltpu.VMEM((2, 1, tq), F32)],
        compiler_params=pltpu.CompilerParams(
            dimension_semantics=("parallel", "parallel", "arbitrary"), vmem_limit_bytes=VMEM_LIMIT),
        name="diff_attn",
    )(qt, qt, k, vt, lam_p, subln_g)


def _rwkv_prep(d, z_ref, pv_ref, wup_ref, aup_ref, gup_ref, seg):
    zs = z_ref[0]

    w = RW_WIDTH
    r, k, v = zs[:, 0:w], zs[:, w:2 * w], zs[:, 2 * w:3 * w]
    wa = zs[:, 3 * w:3 * w + LORA_IN]
    pv = lambda i: pv_ref[i:i + 1]
    w_pre = _dot(jnp.tanh(wa).astype(BF16), wup_ref[d])
    a_pre = _dot(wa.astype(BF16), aup_ref[d])
    lw = -math.exp(-0.5) * jax.nn.sigmoid(pv(PV_W0 + d) + w_pre)
    a_rate = jax.nn.sigmoid(pv(PV_A0 + d) + a_pre)
    kk = k * pv(PV_KK)
    kk = kk * lax.rsqrt(_seg_sum(kk * kk, seg) + KK_EPS)
    kd = k * (1.0 + (a_rate - 1.0) * pv(PV_KA))
    bonus = _seg_sum(r * kd * pv(PV_RK), seg) * v
    out = dict(r=r, v=v, kd=kd, lw=lw, a=-kk, b=kk * a_rate, bonus=bonus)
    if d == 0:
        g_dn = zs[:, 3 * w + LORA_IN:RW_COLS]
        out["g"] = _dot(jax.nn.sigmoid(g_dn).astype(BF16), gup_ref[...])
    return out


def _rwkv_chunks(cis, ops_ref, tri_ref, seg_ref, st_ref, y_refs):
    gw = SEG_WIDTH
    n_groups = RW_HEADS // GROUP_HEADS
    bmask = seg_ref[...]
    ti = lax.broadcasted_iota(jnp.int32, (CHUNK, gw), 0)
    si = lax.broadcasted_iota(jnp.int32, (CHUNK, gw), 1) % HEAD_DIM
    eye = (si == ti).astype(F32)
    strict = ((si < ti), (si > ti))
    incl = ((si <= ti), (si >= ti))
    nt = (((1,), (1,)), ((), ()))
    tn = (((0,), (0,)), ((), ()))
    lane = lax.broadcasted_iota(jnp.int32, (1, LANES), 1)
    half_masks = ((lane < HEAD_DIM).astype(BF16), (lane >= HEAD_DIM).astype(BF16))
    zeros = jnp.zeros((CHUNK, LANES), BF16)

    def blockdiag(x):
        xb = x.astype(BF16)
        blocks = []
        for h in range(GROUP_HEADS):
            part = xb[:, (h // 2) * LANES:(h // 2 + 1) * LANES] * half_masks[h % 2]
            blocks.append(jnp.concatenate([part, zeros] if h < 2 else [zeros, part], axis=1))
        return jnp.concatenate(blocks, axis=0)

    def stack(*xs):
        return jnp.concatenate([x.astype(BF16) for x in xs], axis=0)

    n_seq = len(cis[0])
    rows = [[pl.ds(pl.multiple_of(ci * CHUNK, CHUNK), CHUNK) for ci in cis[d]] for d in range(2)]
    state = {(d, g): st_ref[d, g] for d in range(2) for g in range(n_groups)}
    tails_done = [0]

    def position(k):
        chains = []
        for d in range(2):
            r, v, kd, lw, a, b = (ops_ref[d, n, rows[d][k], :] for n in range(6))
            cl = sum(_dot(tri_ref[d], part) for part in _split2(lw))
            tot = cl[CHUNK - 1:CHUNK] if d == 0 else cl[0:1]
            e_inv = jnp.exp(-cl)
            e_rem = jnp.exp(tot - cl)
            wide = dict(at=a * jnp.exp(cl - lw), rt=r * jnp.exp(cl), bt=b * e_inv, kt=kd * e_inv,
                        bh=b * e_rem, kh=kd * e_rem, v=v)
            wide = {name: x.astype(BF16) for name, x in wide.items()}
            wide["decay"] = jnp.exp(tot)
            for g in range(n_groups):
                c = {name: x[:, g * gw:(g + 1) * gw] for name, x in wide.items()}
                c.update(d=d, g=g)
                chains.append(c)
        yield

        for c in chains:
            ar = stack(c["at"], c["rt"])
            c["ab"] = lax.dot_general(ar, blockdiag(c["bt"]), nt, preferred_element_type=F32)
            c["ak"] = lax.dot_general(ar, blockdiag(c["kt"]), nt, preferred_element_type=F32)
        yield
        for c in chains:
            d = c["d"]
            c["l"] = jnp.where(strict[d], c["ab"][:CHUNK], 0.0)
            c["a_rb"] = jnp.where(incl[d], c["ab"][CHUNK:], 0.0).astype(BF16)
            a_ak = jnp.where(strict[d], c["ak"][:CHUNK], 0.0)
            a_rk = jnp.where(incl[d], c["ak"][CHUNK:], 0.0)
            av = _dot(stack(a_ak, a_rk), blockdiag(c["v"]))
            c["akv"], c["arkv"] = av[:CHUNK].astype(BF16), av[CHUNK:]
        yield

        for c in chains:
            c["t"] = eye + c["l"]
            c["lp"] = _dot(c["l"].astype(BF16), blockdiag(c["l"]))
        yield
        for _ in range(int(math.log2(CHUNK)) - 2):
            for c in chains:
                both = _dot(stack(c["t"], c["lp"]), blockdiag(c["lp"]))
                c["t"] = c["t"] + both[:CHUNK]
                c["lp"] = both[CHUNK:]
            yield
        for c in chains:
            c["t"] = (c["t"] + _dot(c["t"].astype(BF16), blockdiag(c["lp"]))).astype(BF16)
        yield
        for c in chains:
            c["wt"] = _dot(c["t"], blockdiag(c["at"])).astype(BF16)
            c["u_loc"] = _dot(c["t"], blockdiag(c["akv"]))
        yield

        assert tails_done[0] == k, "the skew must let the previous position finish its state update first"
        for c in chains:
            wr = _dot(stack(c["wt"], c["rt"]), blockdiag(state[c["d"], c["g"]]))
            c["u"] = wr[:CHUNK] + c["u_loc"]
            c["y"] = wr[CHUNK:] + c["arkv"]
        yield
        for c in chains:
            c["y"] = c["y"] + _dot(c["a_rb"], blockdiag(c["u"]))
            st_hi, st_lo = _split2(state[c["d"], c["g"]])
            decay = eye * c["decay"]
            full = lax.dot_general(stack(c["bh"], c["kh"], decay, decay), stack(c["u"], c["v"], st_hi, st_lo),
                                   tn, preferred_element_type=F32) * bmask
            state[c["d"], c["g"]] = sum(full[h * HEAD_DIM:(h + 1) * HEAD_DIM] for h in range(GROUP_HEADS))
        tails_done[0] = k + 1
        for d in range(2):
            y_refs[d][0, rows[d][k], :] = jnp.concatenate([c["y"] for c in chains if c["d"] == d], axis=1)

    running = {k: position(k) for k in range(n_seq)}
    rnd = 0
    while running:
        for k in sorted(running):
            if rnd >= k * CHUNK_SKEW and next(running[k], StopIteration) is StopIteration:
                del running[k]
        rnd += 1
    for (d, g), st in state.items():
        st_ref[d, g] = st


def _rwkv_kernel(zf_ref, zb_ref, pv_ref, wup_ref, aup_ref, gup_ref, seg_ref, tri_ref,
                 y0_ref, y1_ref, bon0_ref, bon1_ref, g_ref, ops_ref, st_ref):
    tm = zf_ref.shape[1]

    @pl.when(pl.program_id(1) == 0)
    def _():
        st_ref[...] = jnp.zeros_like(st_ref)

    seg = seg_ref[...]
    for d, (z_ref, bon_ref) in enumerate(((zf_ref, bon0_ref), (zb_ref, bon1_ref))):
        p = _rwkv_prep(d, z_ref, pv_ref, wup_ref, aup_ref, gup_ref, seg)
        for n, name in enumerate(("r", "v", "kd", "lw", "a", "b")):
            ops_ref[d, n] = p[name]
        bon_ref[0] = p["bonus"]
        if d == 0:
            g_ref[0] = p["g"]

    n_chunks = tm // CHUNK
    per_iter = min(CHUNKS_PER_ITER, n_chunks)

    def body(it, carry):
        fwd = [it * per_iter + k for k in range(per_iter)]
        bwd = [n_chunks - 1 - c for c in fwd]
        _rwkv_chunks((fwd, bwd), ops_ref, tri_ref, seg_ref, st_ref, (y0_ref, y1_ref))
        return carry

    lax.fori_loop(0, n_chunks // per_iter, body, 0)


def _rwkv_scan(z, pvec, wup, aup, gup, seg, tri, tm):
    B, S, _ = z.shape
    nt = S // tm
    const = lambda shape: pl.BlockSpec(shape, lambda b, i: (0,) * len(shape))
    fwd = lambda width: pl.BlockSpec((1, tm, width), lambda b, i: (b, i, 0))
    bwd = lambda width: pl.BlockSpec((1, tm, width), lambda b, i: (b, nt - 1 - i, 0))
    tok = jax.ShapeDtypeStruct((B, S, RW_WIDTH), F32)
    return pl.pallas_call(
        _rwkv_kernel,
        grid=(B, nt),
        in_specs=[fwd(RW_COLS), bwd(RW_COLS), const((PV_ROWS, RW_WIDTH)),
                  const((2, LORA_IN, RW_WIDTH)), const((2, LORA_IN, RW_WIDTH)), const((GATE_LORA, RW_WIDTH)),
                  const((SEG_WIDTH, SEG_WIDTH)), const((2, CHUNK, CHUNK))],
        out_specs=[fwd(RW_WIDTH), bwd(RW_WIDTH), fwd(RW_WIDTH), bwd(RW_WIDTH), fwd(RW_WIDTH)],
        out_shape=[tok] * 5,
        scratch_shapes=[pltpu.VMEM((2, 6, tm, RW_WIDTH), F32),
                        pltpu.VMEM((2, RW_HEADS // GROUP_HEADS, HEAD_DIM, GROUP_HEADS * HEAD_DIM), F32)],
        compiler_params=pltpu.CompilerParams(
            dimension_semantics=("arbitrary", "arbitrary"), vmem_limit_bytes=VMEM_LIMIT),
        name="rwkv_scan",
    )(z, z, pvec, wup, aup, gup, seg, tri)


def _out_ffn_kernel(x_ref, oda_ref, y0_ref, y1_ref, bon0_ref, bon1_ref, g_ref, pv_ref, seg_ref,
                    wout_ref, g2_ref, w1_ref, w2_ref, o_ref):
    seg = seg_ref[...]
    y = y0_ref[0] + y1_ref[0]
    mean = _seg_sum(y, seg) * (1.0 / HEAD_DIM)
    yc = y - mean
    var = _seg_sum(yc * yc, seg) * (1.0 / HEAD_DIM)
    yn = yc * lax.rsqrt(var + LN_X_EPS) * pv_ref[PV_LNG:PV_LNG + 1] + pv_ref[PV_LNB:PV_LNB + 1]
    o_rw = (yn + bon0_ref[0] + bon1_ref[0]) * g_ref[0]
    x = (x_ref[0] + _dot(oda_ref[0].astype(BF16), wout_ref[0:DA_WIDTH, :])
         + _dot(o_rw.astype(BF16), wout_ref[DA_WIDTH:D_MODEL, :]))
    ms = jnp.mean(x * x, axis=-1, keepdims=True)
    h = (x * lax.rsqrt(ms + NORM_EPS) * g2_ref[...]).astype(BF16)
    ffn = None
    for c0 in range(0, D_FF, FF_CHUNK):
        u = jnp.maximum(_dot(h, w1_ref[:, c0:c0 + FF_CHUNK]), 0.0)
        part = _dot((u * u).astype(BF16), w2_ref[c0:c0 + FF_CHUNK, :])
        ffn = part if ffn is None else ffn + part
    o_ref[0] = x + ffn


def _out_ffn(x, o_da, y0, y1, bon0, bon1, g, pvec, seg, w_out, g2, w1, w2, tm):
    B, S, _ = x.shape
    const = lambda shape: pl.BlockSpec(shape, lambda b, i: (0,) * len(shape))
    tok = lambda width: pl.BlockSpec((1, tm, width), lambda b, i: (b, i, 0))
    return pl.pallas_call(
        _out_ffn_kernel,
        grid=(B, S // tm),
        in_specs=[tok(D_MODEL)] + [tok(RW_WIDTH)] * 6 + [
            const((PV_ROWS, RW_WIDTH)), const((SEG_WIDTH, SEG_WIDTH)), const((D_MODEL, D_MODEL)),
            const((1, D_MODEL)), const((D_MODEL, D_FF)), const((D_FF, D_MODEL))],
        out_specs=tok(D_MODEL),
        out_shape=jax.ShapeDtypeStruct((B, S, D_MODEL), F32),
        compiler_params=pltpu.CompilerParams(
            dimension_semantics=("parallel", "parallel"), vmem_limit_bytes=VMEM_LIMIT),
        name="out_ffn",
    )(x, o_da, y0, y1, bon0, bon1, g, pvec, seg, w_out, g2, w1, w2)


def _rope_tables(seq_len):
    inv_freq = 1.0 / (ROPE_THETA ** (jnp.arange(0, HEAD_DIM, 2, dtype=F32) / HEAD_DIM))
    half = HEAD_DIM // 2
    inv_lanes = jnp.tile(inv_freq, LANES // half)
    sign = jnp.tile(jnp.concatenate([-jnp.ones(half, F32), jnp.ones(half, F32)]), LANES // HEAD_DIM)
    ang = jnp.arange(seq_len, dtype=F32)[:, None] * inv_lanes[None, :]
    return jnp.cos(ang), jnp.sin(ang) * sign


def _attn_tiles(seq_len):
    tq = 256 if seq_len >= 8192 else _pick(seq_len, 512)
    tk = _pick(seq_len // 2, 512)
    n_blocks = seq_len // tk
    return tq, tk, n_blocks if n_blocks % 2 == 0 and n_blocks <= 32 else 2


def _pick(n, target):
    t = min(n, target)
    assert n % t == 0, (n, t)
    return t


def kernel(x_prompt, x_sample, norm1_g, w_in, q_norm_g, k_norm_g, lam_q1, lam_k1, lam_q2, lam_k2, subln_g,
           mu_prev, mu_next, w0, w_up, a0, a_up, g_up, k_k, k_a, r_k, ln_x_g, ln_x_b, w_out, norm2_g,
           w_ff1, w_ff2):
    l = 0
    w_in_b = w_in[l].astype(BF16)
    w_out_b = w_out[l].astype(BF16)
    w1_b = w_ff1[l].astype(BF16)
    w2_b = w_ff2[l].astype(BF16)
    g1 = norm1_g[l][None, :]
    g2 = norm2_g[l][None, :]
    qg = jnp.tile(q_norm_g[l], DA_WIDTH // HEAD_DIM)[None, :]
    kg = jnp.tile(k_norm_g[l], DA_WIDTH // HEAD_DIM)[None, :]
    lam_p = jnp.stack([lam_q1[l], lam_k1[l], lam_q2[l], lam_k2[l]])
    sg = subln_g[l][None, :]
    mu = jnp.stack([mu_prev[l], mu_next[l]])
    rows = [w0[l, 0], w0[l, 1], a0[l, 0], a0[l, 1], k_k[l], k_a[l], r_k[l].reshape(-1), ln_x_g[l], ln_x_b[l]]
    pvec = jnp.zeros((PV_ROWS, RW_WIDTH), F32).at[:len(rows)].set(jnp.stack(rows))
    zpad = jnp.zeros((2, DECAY_LORA, RW_WIDTH), F32)
    wup = jnp.concatenate([w_up[l], zpad], axis=1).astype(BF16)
    aup = jnp.concatenate([zpad, a_up[l]], axis=1).astype(BF16)
    gup = g_up[l].astype(BF16)
    ch = jnp.arange(SEG_WIDTH) // HEAD_DIM
    seg = (ch[:, None] == ch[None, :]).astype(BF16)
    t = jnp.arange(CHUNK)
    tri = jnp.stack([t[None, :] <= t[:, None], t[None, :] >= t[:, None]]).astype(BF16)

    cos_t, sin_t = _rope_tables(max(x_prompt.shape[1], x_sample.shape[1]))

    def run(x):
        S = x.shape[1]
        tm = _pick(S, TOKEN_TILE)
        qt, k, vt, z_rw = _in_proj(x, g1, w_in_b, qg, kg, cos_t, sin_t, seg, mu, tm)
        tq, tk, per_iter = _attn_tiles(S)
        o_da = _diff_attn(qt, k, vt, lam_p, sg, tq, tk, per_iter)
        y0, y1, bon0, bon1, g = _rwkv_scan(z_rw, pvec, wup, aup, gup, seg, tri, _pick(S, SCAN_TILE))
        return _out_ffn(x, o_da, y0, y1, bon0, bon1, g, pvec, seg, w_out_b, g2, w1_b, w2_b, tm)

    return (run(x_prompt), run(x_sample))
```

```python
import functools
import math

import jax
import jax.numpy as jnp
from jax import lax
from jax.experimental import pallas as pl
from jax.experimental.pallas import tpu as pltpu

F32 = jnp.float32
BF16 = jnp.bfloat16

D_MODEL = 1024
DA_HEADS = 4
HEAD_DIM = 64
DA_V_DIM = 128
DA_WIDTH = DA_HEADS * DA_V_DIM
RW_WIDTH = D_MODEL - DA_WIDTH
RW_HEADS = RW_WIDTH // HEAD_DIM
DECAY_LORA = 64
ICLR_LORA = 64
GATE_LORA = 128
LORA_IN = DECAY_LORA + ICLR_LORA
RW_COLS = 3 * RW_WIDTH + LORA_IN + GATE_LORA
DA_COLS = 3 * DA_WIDTH
IN_COLS = DA_COLS + RW_COLS
D_FF = 4 * D_MODEL
ROPE_THETA = 10000.0
NORM_EPS = 1e-6
LN_X_EPS = 64e-5
KK_EPS = 1e-12
LAMBDA_INIT = 0.8 - 0.6 * math.exp(-0.3 * 0)
LOG2E = 1.4426950408889634
QK_SCALE = HEAD_DIM ** -0.5

LANES = 128
BF16_SUBLANES = 16
CHUNK = 64
GROUP_HEADS = 4
SEG_WIDTH = GROUP_HEADS * HEAD_DIM
CHUNKS_PER_ITER = 4
SCAN_TILE = CHUNK * CHUNKS_PER_ITER
TOKEN_TILE = 512
FF_CHUNK = 1024
VMEM_LIMIT = 56 * 1024 * 1024

PV_W0, PV_A0, PV_KK, PV_KA, PV_RK, PV_LNG, PV_LNB = 0, 2, 4, 5, 6, 7, 8
PV_ROWS = 16


def _dot(a, b):
    return jnp.dot(a, b, preferred_element_type=F32)


def _split2(x):
    hi = x.astype(BF16)
    lo = (x - hi.astype(F32)).astype(BF16)
    return hi, lo


def _seg_sum(x, seg):
    xb = x.astype(BF16)
    gw = seg.shape[0]
    return jnp.concatenate([_dot(xb[:, g * gw:(g + 1) * gw], seg) for g in range(x.shape[1] // gw)], axis=1)


def _in_proj_kernel(x_ref, xp_ref, xn_ref, g1_ref, w_ref, qg_ref, kg_ref, cos_ref, sin_ref, seg_ref, mu_ref,
                    qt_ref, k_ref, vt_ref, z_ref):
    def norm1(x):
        ms = jnp.mean(x * x, axis=-1, keepdims=True)
        return (x * lax.rsqrt(ms + NORM_EPS) * g1_ref[...]).astype(BF16)

    h = norm1(x_ref[0])
    tm = h.shape[0]
    seg = seg_ref[...]
    reps = DA_WIDTH // LANES
    cos = jnp.concatenate([cos_ref[...]] * reps, axis=1)
    sin = jnp.concatenate([sin_ref[...]] * reps, axis=1)
    lane = lax.broadcasted_iota(jnp.int32, (1, LANES), 1)
    first_half = (lane % HEAD_DIM) < (HEAD_DIM // 2)

    def head_norm_rope(z, g):
        ss = _seg_sum(z * z, seg) * (1.0 / HEAD_DIM)
        zn = z * lax.rsqrt(ss + NORM_EPS) * g
        parts = []
        for c in range(reps):
            zc = zn[:, c * LANES:(c + 1) * LANES]
            parts.append(jnp.where(first_half,
                                   pltpu.roll(zc, LANES - HEAD_DIM // 2, 1),
                                   pltpu.roll(zc, HEAD_DIM // 2, 1)))
        rot = jnp.concatenate(parts, axis=1)
        return zn * cos + rot * sin

    zq = _dot(h, w_ref[:, 0:DA_WIDTH])
    qt_ref[0] = (head_norm_rope(zq, qg_ref[...]) * (QK_SCALE * LOG2E)).T.astype(BF16)
    zk = _dot(h, w_ref[:, DA_WIDTH:2 * DA_WIDTH])
    k_ref[0] = head_norm_rope(zk, kg_ref[...]).astype(BF16)
    vt_ref[0] = _dot(h, w_ref[:, 2 * DA_WIDTH:DA_COLS]).T.astype(BF16)

    i = pl.program_id(1)
    h_ext = jnp.concatenate([h, norm1(xp_ref[0]), norm1(xn_ref[0])], axis=0)
    row = lax.broadcasted_iota(jnp.int32, (tm, 1), 0)
    slab = lambda c0: _dot(h_ext, w_ref[:, DA_COLS + c0:DA_COLS + c0 + SEG_WIDTH])
    z_next_group = slab(0)
    for c0 in range(0, RW_COLS, SEG_WIDTH):
        cols = slice(c0, c0 + SEG_WIDTH)
        z_ext = z_next_group
        if c0 + SEG_WIDTH < RW_COLS:
            z_next_group = slab(c0 + SEG_WIDTH)
        z = z_ext[:tm]
        prev_row = jnp.where(i > 0, z_ext[tm + 7:tm + 8], 0.0)
        next_row = jnp.where(i < pl.num_programs(1) - 1, z_ext[tm + 8:tm + 9], 0.0)
        z_prev = jnp.where(row == 0, prev_row, pltpu.roll(z, 1, 0))
        z_next = jnp.where(row == tm - 1, next_row, pltpu.roll(z, tm - 1, 0))
        z_ref[0, :, cols] = z + mu_ref[0:1, cols] * (z_prev - z) + mu_ref[1:2, cols] * (z_next - z)


def _in_proj(x, g1, w_in, qg, kg, cos_t, sin_t, seg, mu, tm):
    B, S, _ = x.shape
    const = lambda shape: pl.BlockSpec(shape, lambda b, i: (0,) * len(shape))
    tok = lambda width: pl.BlockSpec((1, tm, width), lambda b, i: (b, i, 0))
    tok_t = pl.BlockSpec((1, DA_WIDTH, tm), lambda b, i: (b, 0, i))
    halo_prev = pl.BlockSpec((1, 8, D_MODEL), lambda b, i: (b, jnp.maximum(i * (tm // 8) - 1, 0), 0))
    halo_next = pl.BlockSpec((1, 8, D_MODEL), lambda b, i: (b, jnp.minimum((i + 1) * (tm // 8), S // 8 - 1), 0))
    return pl.pallas_call(
        _in_proj_kernel,
        grid=(B, S // tm),
        in_specs=[tok(D_MODEL), halo_prev, halo_next, const((1, D_MODEL)), const((D_MODEL, IN_COLS)),
                  const((1, DA_WIDTH)), const((1, DA_WIDTH)),
                  pl.BlockSpec((tm, LANES), lambda b, i: (i, 0)),
                  pl.BlockSpec((tm, LANES), lambda b, i: (i, 0)),
                  const((SEG_WIDTH, SEG_WIDTH)), const((2, RW_COLS))],
        out_specs=[tok_t, tok(DA_WIDTH), tok_t, tok(RW_COLS)],
        out_shape=[jax.ShapeDtypeStruct((B, DA_WIDTH, S), BF16), jax.ShapeDtypeStruct((B, S, DA_WIDTH), BF16),
                   jax.ShapeDtypeStruct((B, DA_WIDTH, S), BF16), jax.ShapeDtypeStruct((B, S, RW_COLS), F32)],
        compiler_params=pltpu.CompilerParams(
            dimension_semantics=("parallel", "parallel"), vmem_limit_bytes=VMEM_LIMIT),
        name="in_proj",
    )(x, x, x, g1, w_in, qg, kg, cos_t, sin_t, seg, mu)


def _diff_attn_kernel(qt_ref, qtn_ref, k_ref, vt_ref, lam_ref, sg_ref, o_ref, s_ref, acc_ref, mb_ref, *, tk,
                      per_iter):
    tq = qt_ref.shape[2]
    nk = k_ref.shape[1] // tk
    row = lax.broadcasted_iota(jnp.int32, (LANES, 1), 0)

    def components(qt):
        zero = jnp.zeros_like(qt)
        return jnp.where(row < HEAD_DIM, qt, zero), jnp.where(row >= HEAD_DIM, qt, zero)

    qts = components(qt_ref[0])
    acc_ref[...] = jnp.zeros_like(acc_ref)

    def scores(slot, blk, q_pair):
        kb = k_ref[0, pl.ds(pl.multiple_of(blk * tk, tk), tk), :]
        mblk = []
        for c in range(2):
            s = _dot(kb, q_pair[c])
            s_ref[slot, c] = s
            mblk.append(jnp.max(s, axis=0, keepdims=True))
        return tuple(mblk)

    @pl.when(pl.program_id(2) == 0)
    def _():
        first = scores(0, 0, qts)
        for c in range(2):
            mb_ref[c] = first[c]

    def consume(slot, blk, mblk, ml):
        vtb = vt_ref[0, :, pl.ds(pl.multiple_of(blk * tk, tk), tk)]
        vtb_ones = jnp.concatenate([vtb, jnp.ones((BF16_SUBLANES, tk), BF16)], axis=0)
        out = []
        for c in range(2):
            m, l = ml[c]
            m_new = jnp.maximum(m, mblk[c])
            alpha = jnp.exp2(m - m_new)
            p = jnp.exp2(s_ref[slot, c] - m_new)
            pv = _dot(vtb_ones, p.astype(BF16))
            l = alpha * l + pv[DA_V_DIM:DA_V_DIM + 1]
            acc_ref[c] = alpha * acc_ref[c] + pv[:DA_V_DIM]
            out.append((m_new, l))
        return tuple(out)

    def steps(first, carry, last):
        mblk, ml = carry
        for u in range(per_iter):
            if last and u == per_iter - 1:
                nxt = scores(0, 0, components(qtn_ref[0]))
            else:
                nxt = scores((u + 1) % 2, first + u + 1, qts)
            ml = consume(u % 2, first + u, mblk, ml)
            mblk = nxt
        return mblk, ml

    ml = tuple((jnp.full((1, tq), -1e30, F32), jnp.zeros((1, tq), F32)) for _ in range(2))
    carry = lax.fori_loop(0, nk // per_iter - 1, lambda i, c: steps(i * per_iter, c, False),
                          ((mb_ref[0], mb_ref[1]), ml))
    mb_next, ((_, l0), (_, l1)) = steps(nk - per_iter, carry, True)
    for c in range(2):
        mb_ref[c] = mb_next[c]

    lp = lam_ref[...]
    lam = (jnp.exp(jnp.sum(lp[0:1] * lp[1:2], axis=1, keepdims=True))
           - jnp.exp(jnp.sum(lp[2:3] * lp[3:4], axis=1, keepdims=True)) + LAMBDA_INIT)
    o = (acc_ref[0] / l0 - lam * (acc_ref[1] / l1)).T
    ms = jnp.mean(o * o, axis=-1, keepdims=True)
    o_ref[0] = o * lax.rsqrt(ms + NORM_EPS) * sg_ref[...] * (1.0 - LAMBDA_INIT)


def _diff_attn(qt, k, vt, lam_p, subln_g, tq, tk, per_iter):
    B, S, _ = k.shape
    assert per_iter % 2 == 0 and (S // tk) % per_iter == 0
    nq = S // tq
    return pl.pallas_call(
        functools.partial(_diff_attn_kernel, tk=tk, per_iter=per_iter),
        grid=(B, DA_HEADS, nq),
        in_specs=[pl.BlockSpec((1, LANES, tq), lambda b, h, i: (b, h, i)),
                  pl.BlockSpec((1, LANES, tq), lambda b, h, i: (b, h, jnp.minimum(i + 1, nq - 1))),
                  pl.BlockSpec((1, S, LANES), lambda b, h, i: (b, 0, h)),
                  pl.BlockSpec((1, LANES, S), lambda b, h, i: (b, h, 0)),
                  pl.BlockSpec((4, HEAD_DIM), lambda b, h, i: (0, 0)),
                  pl.BlockSpec((1, DA_V_DIM), lambda b, h, i: (0, 0))],
        out_specs=pl.BlockSpec((1, tq, DA_V_DIM), lambda b, h, i: (b, i, h)),
        out_shape=jax.ShapeDtypeStruct((B, S, DA_WIDTH), F32),
        scratch_shapes=[pltpu.VMEM((2, 2, tk, tq), F32), pltpu.VMEM((2, DA_V_DIM, tq), F32),
                        pltpu.VMEM((2, 1, tq), F32)],
        compiler_params=pltpu.CompilerParams(
            dimension_semantics=("parallel", "parallel", "arbitrary"), vmem_limit_bytes=VMEM_LIMIT),
        name="diff_attn",
    )(qt, qt, k, vt, lam_p, subln_g)


def _rwkv_prep(d, z_ref, pv_ref, wup_ref, aup_ref, gup_ref, seg):
    zs = z_ref[0]

    w = RW_WIDTH
    r, k, v = zs[:, 0:w], zs[:, w:2 * w], zs[:, 2 * w:3 * w]
    wa = zs[:, 3 * w:3 * w + LORA_IN]
    pv = lambda i: pv_ref[i:i + 1]
    w_pre = _dot(jnp.tanh(wa).astype(BF16), wup_ref[d])
    a_pre = _dot(wa.astype(BF16), aup_ref[d])
    lw = -math.exp(-0.5) * jax.nn.sigmoid(pv(PV_W0 + d) + w_pre)
    a_rate = jax.nn.sigmoid(pv(PV_A0 + d) + a_pre)
    kk = k * pv(PV_KK)
    kk = kk * lax.rsqrt(_seg_sum(kk * kk, seg) + KK_EPS)
    kd = k * (1.0 + (a_rate - 1.0) * pv(PV_KA))
    bonus = _seg_sum(r * kd * pv(PV_RK), seg) * v
    out = dict(r=r, v=v, kd=kd, lw=lw, a=-kk, b=kk * a_rate, bonus=bonus)
    if d == 0:
        g_dn = zs[:, 3 * w + LORA_IN:RW_COLS]
        out["g"] = _dot(jax.nn.sigmoid(g_dn).astype(BF16), gup_ref[...])
    return out


def _rwkv_chunks(cis, ops_ref, tri_ref, seg_ref, st_ref, y_refs):
    gw = SEG_WIDTH
    n_groups = RW_HEADS // GROUP_HEADS
    bmask = seg_ref[...]
    ti = lax.broadcasted_iota(jnp.int32, (CHUNK, gw), 0)
    si = lax.broadcasted_iota(jnp.int32, (CHUNK, gw), 1) % HEAD_DIM
    eye = (si == ti).astype(F32)
    strict = ((si < ti), (si > ti))
    incl = ((si <= ti), (si >= ti))
    nt = (((1,), (1,)), ((), ()))
    tn = (((0,), (0,)), ((), ()))
    lane = lax.broadcasted_iota(jnp.int32, (1, LANES), 1)
    half_masks = ((lane < HEAD_DIM).astype(BF16), (lane >= HEAD_DIM).astype(BF16))
    zeros = jnp.zeros((CHUNK, LANES), BF16)

    def blockdiag(x):
        xb = x.astype(BF16)
        blocks = []
        for h in range(GROUP_HEADS):
            part = xb[:, (h // 2) * LANES:(h // 2 + 1) * LANES] * half_masks[h % 2]
            blocks.append(jnp.concatenate([part, zeros] if h < 2 else [zeros, part], axis=1))
        return jnp.concatenate(blocks, axis=0)

    def stack(*xs):
        return jnp.concatenate([x.astype(BF16) for x in xs], axis=0)

    n_seq = len(cis[0])
    rows = [[pl.ds(pl.multiple_of(ci * CHUNK, CHUNK), CHUNK) for ci in cis[d]] for d in range(2)]
    state = {(d, g): st_ref[d, g] for d in range(2) for g in range(n_groups)}
    tails_done = [0]

    def position(k):
        chains = []
        for d in range(2):
            r, v, kd, lw, a, b = (ops_ref[d, n, rows[d][k], :] for n in range(6))
            cl = sum(_dot(tri_ref[d], part) for part in _split2(lw))
            tot = cl[CHUNK - 1:CHUNK] if d == 0 else cl[0:1]
            e_inv = jnp.exp(-cl)
            e_rem = jnp.exp(tot - cl)
            wide = dict(at=a * jnp.exp(cl - lw), rt=r * jnp.exp(cl), bt=b * e_inv, kt=kd * e_inv,
                        bh=b * e_rem, kh=kd * e_rem, v=v)
            wide = {name: x.astype(BF16) for name, x in wide.items()}
            wide["decay"] = jnp.exp(tot)
            for g in range(n_groups):
                c = {name: x[:, g * gw:(g + 1) * gw] for name, x in wide.items()}
                c.update(d=d, g=g)
                chains.append(c)
        yield

        for c in chains:
            ar = stack(c["at"], c["rt"])
            c["ab"] = lax.dot_general(ar, blockdiag(c["bt"]), nt, preferred_element_type=F32)
            c["ak"] = lax.dot_general(ar, blockdiag(c["kt"]), nt, preferred_element_type=F32)
        yield
        for c in chains:
            d = c["d"]
            c["l"] = jnp.where(strict[d], c["ab"][:CHUNK], 0.0)
            c["a_rb"] = jnp.where(incl[d], c["ab"][CHUNK:], 0.0).astype(BF16)
            a_ak = jnp.where(strict[d], c["ak"][:CHUNK], 0.0)
            a_rk = jnp.where(incl[d], c["ak"][CHUNK:], 0.0)
            av = _dot(stack(a_ak, a_rk), blockdiag(c["v"]))
            c["akv"], c["arkv"] = av[:CHUNK].astype(BF16), av[CHUNK:]
        yield

        for c in chains:
            c["t"] = eye + c["l"]
            c["lp"] = _dot(c["l"].astype(BF16), blockdiag(c["l"]))
        yield
        for _ in range(int(math.log2(CHUNK)) - 2):
            for c in chains:
                both = _dot(stack(c["t"], c["lp"]), blockdiag(c["lp"]))
                c["t"] = c["t"] + both[:CHUNK]
                c["lp"] = both[CHUNK:]
            yield
        for c in chains:
            c["t"] = (c["t"] + _dot(c["t"].astype(BF16), blockdiag(c["lp"]))).astype(BF16)
        yield
        for c in chains:
            c["wt"] = _dot(c["t"], blockdiag(c["at"])).astype(BF16)
            c["u_loc"] = _dot(c["t"], blockdiag(c["akv"]))
        yield "tail"

        assert tails_done[0] == k
        for c in chains:
            wr = _dot(stack(c["wt"], c["rt"]), blockdiag(state[c["d"], c["g"]]))
            c["u"] = wr[:CHUNK] + c["u_loc"]
            c["y"] = wr[CHUNK:] + c["arkv"]
        yield
        for c in chains:
            c["y"] = c["y"] + _dot(c["a_rb"], blockdiag(c["u"]))
            st_hi, st_lo = _split2(state[c["d"], c["g"]])
            decay = eye * c["decay"]
            full = lax.dot_general(stack(c["bh"], c["kh"], decay, decay), stack(c["u"], c["v"], st_hi, st_lo),
                                   tn, preferred_element_type=F32) * bmask
            state[c["d"], c["g"]] = sum(full[h * HEAD_DIM:(h + 1) * HEAD_DIM] for h in range(GROUP_HEADS))
        tails_done[0] = k + 1
        for d in range(2):
            y_refs[d][0, rows[d][k], :] = jnp.concatenate([c["y"] for c in chains if c["d"] == d], axis=1)

    positions = [position(k) for k in range(n_seq)]
    while all([next(p) != "tail" for p in positions]):
        pass
    for p in positions:
        for _ in p:
            pass
    for (d, g), st in state.items():
        st_ref[d, g] = st


def _rwkv_kernel(zf_ref, zb_ref, pv_ref, wup_ref, aup_ref, gup_ref, seg_ref, tri_ref,
                 y0_ref, y1_ref, bon0_ref, bon1_ref, g_ref, ops_ref, st_ref):
    tm = zf_ref.shape[1]

    @pl.when(pl.program_id(1) == 0)
    def _():
        st_ref[...] = jnp.zeros_like(st_ref)

    seg = seg_ref[...]
    for d, (z_ref, bon_ref) in enumerate(((zf_ref, bon0_ref), (zb_ref, bon1_ref))):
        p = _rwkv_prep(d, z_ref, pv_ref, wup_ref, aup_ref, gup_ref, seg)
        for n, name in enumerate(("r", "v", "kd", "lw", "a", "b")):
            ops_ref[d, n] = p[name]
        bon_ref[0] = p["bonus"]
        if d == 0:
            g_ref[0] = p["g"]

    n_chunks = tm // CHUNK
    per_iter = min(CHUNKS_PER_ITER, n_chunks)

    def body(it, carry):
        fwd = [it * per_iter + k for k in range(per_iter)]
        bwd = [n_chunks - 1 - c for c in fwd]
        _rwkv_chunks((fwd, bwd), ops_ref, tri_ref, seg_ref, st_ref, (y0_ref, y1_ref))
        return carry

    lax.fori_loop(0, n_chunks // per_iter, body, 0)


def _rwkv_scan(z, pvec, wup, aup, gup, seg, tri, tm):
    B, S, _ = z.shape
    nt = S // tm
    const = lambda shape: pl.BlockSpec(shape, lambda b, i: (0,) * len(shape))
    fwd = lambda width: pl.BlockSpec((1, tm, width), lambda b, i: (b, i, 0))
    bwd = lambda width: pl.BlockSpec((1, tm, width), lambda b, i: (b, nt - 1 - i, 0))
    tok = jax.ShapeDtypeStruct((B, S, RW_WIDTH), F32)
    return pl.pallas_call(
        _rwkv_kernel,
        grid=(B, nt),
        in_specs=[fwd(RW_COLS), bwd(RW_COLS), const((PV_ROWS, RW_WIDTH)),
                  const((2, LORA_IN, RW_WIDTH)), const((2, LORA_IN, RW_WIDTH)), const((GATE_LORA, RW_WIDTH)),
                  const((SEG_WIDTH, SEG_WIDTH)), const((2, CHUNK, CHUNK))],
        out_specs=[fwd(RW_WIDTH), bwd(RW_WIDTH), fwd(RW_WIDTH), bwd(RW_WIDTH), fwd(RW_WIDTH)],
        out_shape=[tok] * 5,
        scratch_shapes=[pltpu.VMEM((2, 6, tm, RW_WIDTH), F32),
                        pltpu.VMEM((2, RW_HEADS // GROUP_HEADS, HEAD_DIM, GROUP_HEADS * HEAD_DIM), F32)],
        compiler_params=pltpu.CompilerParams(
            dimension_semantics=("arbitrary", "arbitrary"), vmem_limit_bytes=VMEM_LIMIT),
        name="rwkv_scan",
    )(z, z, pvec, wup, aup, gup, seg, tri)


def _out_ffn_kernel(x_ref, oda_ref, y0_ref, y1_ref, bon0_ref, bon1_ref, g_ref, pv_ref, seg_ref,
                    wout_ref, g2_ref, w1_ref, w2_ref, o_ref):
    seg = seg_ref[...]
    y = y0_ref[0] + y1_ref[0]
    mean = _seg_sum(y, seg) * (1.0 / HEAD_DIM)
    yc = y - mean
    var = _seg_sum(yc * yc, seg) * (1.0 / HEAD_DIM)
    yn = yc * lax.rsqrt(var + LN_X_EPS) * pv_ref[PV_LNG:PV_LNG + 1] + pv_ref[PV_LNB:PV_LNB + 1]
    o_rw = (yn + bon0_ref[0] + bon1_ref[0]) * g_ref[0]
    x = (x_ref[0] + _dot(oda_ref[0].astype(BF16), wout_ref[0:DA_WIDTH, :])
         + _dot(o_rw.astype(BF16), wout_ref[DA_WIDTH:D_MODEL, :]))
    ms = jnp.mean(x * x, axis=-1, keepdims=True)
    h = (x * lax.rsqrt(ms + NORM_EPS) * g2_ref[...]).astype(BF16)
    ffn = None
    for c0 in range(0, D_FF, FF_CHUNK):
        u = jnp.maximum(_dot(h, w1_ref[:, c0:c0 + FF_CHUNK]), 0.0)
        part = _dot((u * u).astype(BF16), w2_ref[c0:c0 + FF_CHUNK, :])
        ffn = part if ffn is None else ffn + part
    o_ref[0] = x + ffn


def _out_ffn(x, o_da, y0, y1, bon0, bon1, g, pvec, seg, w_out, g2, w1, w2, tm):
    B, S, _ = x.shape
    const = lambda shape: pl.BlockSpec(shape, lambda b, i: (0,) * len(shape))
    tok = lambda width: pl.BlockSpec((1, tm, width), lambda b, i: (b, i, 0))
    return pl.pallas_call(
        _out_ffn_kernel,
        grid=(B, S // tm),
        in_specs=[tok(D_MODEL)] + [tok(RW_WIDTH)] * 6 + [
            const((PV_ROWS, RW_WIDTH)), const((SEG_WIDTH, SEG_WIDTH)), const((D_MODEL, D_MODEL)),
            const((1, D_MODEL)), const((D_MODEL, D_FF)), const((D_FF, D_MODEL))],
        out_specs=tok(D_MODEL),
        out_shape=jax.ShapeDtypeStruct((B, S, D_MODEL), F32),
        compiler_params=pltpu.CompilerParams(
            dimension_semantics=("parallel", "parallel"), vmem_limit_bytes=VMEM_LIMIT),
        name="out_ffn",
    )(x, o_da, y0, y1, bon0, bon1, g, pvec, seg, w_out, g2, w1, w2)


def _rope_tables(seq_len):
    inv_freq = 1.0 / (ROPE_THETA ** (jnp.arange(0, HEAD_DIM, 2, dtype=F32) / HEAD_DIM))
    half = HEAD_DIM // 2
    inv_lanes = jnp.tile(inv_freq, LANES // half)
    sign = jnp.tile(jnp.concatenate([-jnp.ones(half, F32), jnp.ones(half, F32)]), LANES // HEAD_DIM)
    ang = jnp.arange(seq_len, dtype=F32)[:, None] * inv_lanes[None, :]
    return jnp.cos(ang), jnp.sin(ang) * sign


def _attn_tiles(seq_len):
    tq = 256 if seq_len >= 8192 else _pick(seq_len, 512)
    tk = _pick(seq_len // 2, 512)
    n_blocks = seq_len // tk
    return tq, tk, n_blocks if n_blocks % 2 == 0 and n_blocks <= 32 else 2


def _pick(n, target):
    t = min(n, target)
    assert n % t == 0, (n, t)
    return t


def kernel(x_prompt, x_sample, norm1_g, w_in, q_norm_g, k_norm_g, lam_q1, lam_k1, lam_q2, lam_k2, subln_g,
           mu_prev, mu_next, w0, w_up, a0, a_up, g_up, k_k, k_a, r_k, ln_x_g, ln_x_b, w_out, norm2_g,
           w_ff1, w_ff2):
    l = 0
    w_in_b = w_in[l].astype(BF16)
    w_out_b = w_out[l].astype(BF16)
    w1_b = w_ff1[l].astype(BF16)
    w2_b = w_ff2[l].astype(BF16)
    g1 = norm1_g[l][None, :]
    g2 = norm2_g[l][None, :]
    qg = jnp.tile(q_norm_g[l], DA_WIDTH // HEAD_DIM)[None, :]
    kg = jnp.tile(k_norm_g[l], DA_WIDTH // HEAD_DIM)[None, :]
    lam_p = jnp.stack([lam_q1[l], lam_k1[l], lam_q2[l], lam_k2[l]])
    sg = subln_g[l][None, :]
    mu = jnp.stack([mu_prev[l], mu_next[l]])
    rows = [w0[l, 0], w0[l, 1], a0[l, 0], a0[l, 1], k_k[l], k_a[l], r_k[l].reshape(-1), ln_x_g[l], ln_x_b[l]]
    pvec = jnp.zeros((PV_ROWS, RW_WIDTH), F32).at[:len(rows)].set(jnp.stack(rows))
    zpad = jnp.zeros((2, DECAY_LORA, RW_WIDTH), F32)
    wup = jnp.concatenate([w_up[l], zpad], axis=1).astype(BF16)
    aup = jnp.concatenate([zpad, a_up[l]], axis=1).astype(BF16)
    gup = g_up[l].astype(BF16)
    ch = jnp.arange(SEG_WIDTH) // HEAD_DIM
    seg = (ch[:, None] == ch[None, :]).astype(BF16)
    t = jnp.arange(CHUNK)
    tri = jnp.stack([t[None, :] <= t[:, None], t[None, :] >= t[:, None]]).astype(BF16)

    cos_t, sin_t = _rope_tables(max(x_prompt.shape[1], x_sample.shape[1]))

    def run(x):
        S = x.shape[1]
        tm = _pick(S, TOKEN_TILE)
        qt, k, vt, z_rw = _in_proj(x, g1, w_in_b, qg, kg, cos_t, sin_t, seg, mu, tm)
        tq, tk, per_iter = _attn_tiles(S)
        o_da = _diff_attn(qt, k, vt, lam_p, sg, tq, tk, per_iter)
        y0, y1, bon0, bon1, g = _rwkv_scan(z_rw, pvec, wup, aup, gup, seg, tri, _pick(S, SCAN_TILE))
        return _out_ffn(x, o_da, y0, y1, bon0, bon1, g, pvec, seg, w_out_b, g2, w1_b, w2_b, tm)

    return (run(x_prompt), run(x_sample))
```

```python
import functools
import math

import jax
import jax.numpy as jnp
from jax import lax
from jax.experimental import pallas as pl
from jax.experimental.pallas import tpu as pltpu

F32 = jnp.float32
BF16 = jnp.bfloat16

D_MODEL = 1024
DA_HEADS = 4
HEAD_DIM = 64
DA_V_DIM = 128
DA_WIDTH = DA_HEADS * DA_V_DIM
RW_WIDTH = D_MODEL - DA_WIDTH
RW_HEADS = RW_WIDTH // HEAD_DIM
DECAY_LORA = 64
ICLR_LORA = 64
GATE_LORA = 128
LORA_IN = DECAY_LORA + ICLR_LORA
RW_COLS = 3 * RW_WIDTH + LORA_IN + GATE_LORA
DA_COLS = 3 * DA_WIDTH
IN_COLS = DA_COLS + RW_COLS
D_FF = 4 * D_MODEL
ROPE_THETA = 10000.0
NORM_EPS = 1e-6
LN_X_EPS = 64e-5
KK_EPS = 1e-12
LAMBDA_INIT = 0.8 - 0.6 * math.exp(-0.3 * 0)
LOG2E = 1.4426950408889634
QK_SCALE = HEAD_DIM ** -0.5

LANES = 128
BF16_SUBLANES = 16
CHUNK = 64
GROUP_HEADS = 4
SEG_WIDTH = GROUP_HEADS * HEAD_DIM
CHUNKS_PER_ITER = 4
SCAN_TILE = CHUNK * CHUNKS_PER_ITER
TOKEN_TILE = 512
FF_CHUNK = 1024
VMEM_LIMIT = 56 * 1024 * 1024

PV_W0, PV_A0, PV_KK, PV_KA, PV_RK, PV_LNG, PV_LNB = 0, 2, 4, 5, 6, 7, 8
PV_ROWS = 16


def _dot(a, b):
    return jnp.dot(a, b, preferred_element_type=F32)


def _split2(x):
    hi = x.astype(BF16)
    lo = (x - hi.astype(F32)).astype(BF16)
    return hi, lo


def _seg_sum(x, seg):
    xb = x.astype(BF16)
    gw = seg.shape[0]
    return jnp.concatenate([_dot(xb[:, g * gw:(g + 1) * gw], seg) for g in range(x.shape[1] // gw)], axis=1)


def _in_proj_kernel(x_ref, xp_ref, xn_ref, g1_ref, w_ref, qg_ref, kg_ref, cos_ref, sin_ref, seg_ref, mu_ref,
                    qt_ref, k_ref, vt_ref, z_ref):
    def norm1(x):
        ms = jnp.mean(x * x, axis=-1, keepdims=True)
        return (x * lax.rsqrt(ms + NORM_EPS) * g1_ref[...]).astype(BF16)

    h = norm1(x_ref[0])
    tm = h.shape[0]
    seg = seg_ref[...]
    reps = DA_WIDTH // LANES
    cos = jnp.concatenate([cos_ref[...]] * reps, axis=1)
    sin = jnp.concatenate([sin_ref[...]] * reps, axis=1)
    lane = lax.broadcasted_iota(jnp.int32, (1, LANES), 1)
    first_half = (lane % HEAD_DIM) < (HEAD_DIM // 2)

    def head_norm_rope(z, g):
        ss = _seg_sum(z * z, seg) * (1.0 / HEAD_DIM)
        zn = z * lax.rsqrt(ss + NORM_EPS) * g
        parts = []
        for c in range(reps):
            zc = zn[:, c * LANES:(c + 1) * LANES]
            parts.append(jnp.where(first_half,
                                   pltpu.roll(zc, LANES - HEAD_DIM // 2, 1),
                                   pltpu.roll(zc, HEAD_DIM // 2, 1)))
        rot = jnp.concatenate(parts, axis=1)
        return zn * cos + rot * sin

    zq = _dot(h, w_ref[:, 0:DA_WIDTH])
    qt_ref[0] = (head_norm_rope(zq, qg_ref[...]) * (QK_SCALE * LOG2E)).T.astype(BF16)
    zk = _dot(h, w_ref[:, DA_WIDTH:2 * DA_WIDTH])
    k_ref[0] = head_norm_rope(zk, kg_ref[...]).astype(BF16)
    vt_ref[0] = _dot(h, w_ref[:, 2 * DA_WIDTH:DA_COLS]).T.astype(BF16)

    i = pl.program_id(1)
    h_ext = jnp.concatenate([h, norm1(xp_ref[0]), norm1(xn_ref[0])], axis=0)
    row = lax.broadcasted_iota(jnp.int32, (tm, 1), 0)
    slab = lambda c0: _dot(h_ext, w_ref[:, DA_COLS + c0:DA_COLS + c0 + SEG_WIDTH])
    z_next_group = slab(0)
    for c0 in range(0, RW_COLS, SEG_WIDTH):
        cols = slice(c0, c0 + SEG_WIDTH)
        z_ext = z_next_group
        if c0 + SEG_WIDTH < RW_COLS:
            z_next_group = slab(c0 + SEG_WIDTH)
        z = z_ext[:tm]
        prev_row = jnp.where(i > 0, z_ext[tm + 7:tm + 8], 0.0)
        next_row = jnp.where(i < pl.num_programs(1) - 1, z_ext[tm + 8:tm + 9], 0.0)
        z_prev = jnp.where(row == 0, prev_row, pltpu.roll(z, 1, 0))
        z_next = jnp.where(row == tm - 1, next_row, pltpu.roll(z, tm - 1, 0))
        z_ref[0, :, cols] = z + mu_ref[0:1, cols] * (z_prev - z) + mu_ref[1:2, cols] * (z_next - z)


def _in_proj(x, g1, w_in, qg, kg, cos_t, sin_t, seg, mu, tm):
    B, S, _ = x.shape
    const = lambda shape: pl.BlockSpec(shape, lambda b, i: (0,) * len(shape))
    tok = lambda width: pl.BlockSpec((1, tm, width), lambda b, i: (b, i, 0))
    tok_t = pl.BlockSpec((1, DA_WIDTH, tm), lambda b, i: (b, 0, i))
    halo_prev = pl.BlockSpec((1, 8, D_MODEL), lambda b, i: (b, jnp.maximum(i * (tm // 8) - 1, 0), 0))
    halo_next = pl.BlockSpec((1, 8, D_MODEL), lambda b, i: (b, jnp.minimum((i + 1) * (tm // 8), S // 8 - 1), 0))
    return pl.pallas_call(
        _in_proj_kernel,
        grid=(B, S // tm),
        in_specs=[tok(D_MODEL), halo_prev, halo_next, const((1, D_MODEL)), const((D_MODEL, IN_COLS)),
                  const((1, DA_WIDTH)), const((1, DA_WIDTH)),
                  pl.BlockSpec((tm, LANES), lambda b, i: (i, 0)),
                  pl.BlockSpec((tm, LANES), lambda b, i: (i, 0)),
                  const((SEG_WIDTH, SEG_WIDTH)), const((2, RW_COLS))],
        out_specs=[tok_t, tok(DA_WIDTH), tok_t, tok(RW_COLS)],
        out_shape=[jax.ShapeDtypeStruct((B, DA_WIDTH, S), BF16), jax.ShapeDtypeStruct((B, S, DA_WIDTH), BF16),
                   jax.ShapeDtypeStruct((B, DA_WIDTH, S), BF16), jax.ShapeDtypeStruct((B, S, RW_COLS), F32)],
        compiler_params=pltpu.CompilerParams(
            dimension_semantics=("parallel", "parallel"), vmem_limit_bytes=VMEM_LIMIT),
        name="in_proj",
    )(x, x, x, g1, w_in, qg, kg, cos_t, sin_t, seg, mu)


def _diff_attn_kernel(qt_ref, qtn_ref, k_ref, vt_ref, lam_ref, sg_ref, o_ref, s_ref, acc_ref, mb_ref, *, tk,
                      per_iter):
    tq = qt_ref.shape[2]
    nk = k_ref.shape[1] // tk
    row = lax.broadcasted_iota(jnp.int32, (LANES, 1), 0)

    def components(qt):
        zero = jnp.zeros_like(qt)
        return jnp.where(row < HEAD_DIM, qt, zero), jnp.where(row >= HEAD_DIM, qt, zero)

    qts = components(qt_ref[0])
    acc_ref[...] = jnp.zeros_like(acc_ref)

    def scores(slot, blk, q_pair):
        kb = k_ref[0, pl.ds(pl.multiple_of(blk * tk, tk), tk), :]
        mblk = []
        for c in range(2):
            s = _dot(kb, q_pair[c])
            s_ref[slot, c] = s
            mblk.append(jnp.max(s, axis=0, keepdims=True))
        return tuple(mblk)

    @pl.when(pl.program_id(2) == 0)
    def _():
        first = scores(0, 0, qts)
        for c in range(2):
            mb_ref[c] = first[c]

    def consume(slot, blk, mblk, ml):
        vtb = vt_ref[0, :, pl.ds(pl.multiple_of(blk * tk, tk), tk)]
        vtb_ones = jnp.concatenate([vtb, jnp.ones((BF16_SUBLANES, tk), BF16)], axis=0)
        out = []
        for c in range(2):
            m, l = ml[c]
            m_new = jnp.maximum(m, mblk[c])
            alpha = jnp.exp2(m - m_new)
            p = jnp.exp2(s_ref[slot, c] - m_new)
            pv = _dot(vtb_ones, p.astype(BF16))
            l = alpha * l + pv[DA_V_DIM:DA_V_DIM + 1]
            acc_ref[c] = alpha * acc_ref[c] + pv[:DA_V_DIM]
            out.append((m_new, l))
        return tuple(out)

    def steps(first, carry, last):
        mblk, ml = carry
        for u in range(per_iter):
            if last and u == per_iter - 1:
                nxt = scores(0, 0, components(qtn_ref[0]))
            else:
                nxt = scores((u + 1) % 2, first + u + 1, qts)
            ml = consume(u % 2, first + u, mblk, ml)
            mblk = nxt
        return mblk, ml

    ml = tuple((jnp.full((1, tq), -1e30, F32), jnp.zeros((1, tq), F32)) for _ in range(2))
    carry = lax.fori_loop(0, nk // per_iter - 1, lambda i, c: steps(i * per_iter, c, False),
                          ((mb_ref[0], mb_ref[1]), ml))
    mb_next, ((_, l0), (_, l1)) = steps(nk - per_iter, carry, True)
    for c in range(2):
        mb_ref[c] = mb_next[c]

    lp = lam_ref[...]
    lam = (jnp.exp(jnp.sum(lp[0:1] * lp[1:2], axis=1, keepdims=True))
           - jnp.exp(jnp.sum(lp[2:3] * lp[3:4], axis=1, keepdims=True)) + LAMBDA_INIT)
    o = (acc_ref[0] / l0 - lam * (acc_ref[1] / l1)).T
    ms = jnp.mean(o * o, axis=-1, keepdims=True)
    o_ref[0] = o * lax.rsqrt(ms + NORM_EPS) * sg_ref[...] * (1.0 - LAMBDA_INIT)


def _diff_attn(qt, k, vt, lam_p, subln_g, tq, tk, per_iter):
    B, S, _ = k.shape
    assert per_iter % 2 == 0 and (S // tk) % per_iter == 0
    nq = S // tq
    return pl.pallas_call(
        functools.partial(_diff_attn_kernel, tk=tk, per_iter=per_iter),
        grid=(B, DA_HEADS, nq),
        in_specs=[pl.BlockSpec((1, LANES, tq), lambda b, h, i: (b, h, i)),
                  pl.BlockSpec((1, LANES, tq), lambda b, h, i: (b, h, jnp.minimum(i + 1, nq - 1))),
                  pl.BlockSpec((1, S, LANES), lambda b, h, i: (b, 0, h)),
                  pl.BlockSpec((1, LANES, S), lambda b, h, i: (b, h, 0)),
                  pl.BlockSpec((4, HEAD_DIM), lambda b, h, i: (0, 0)),
                  pl.BlockSpec((1, DA_V_DIM), lambda b, h, i: (0, 0))],
        out_specs=pl.BlockSpec((1, tq, DA_V_DIM), lambda b, h, i: (b, i, h)),
        out_shape=jax.ShapeDtypeStruct((B, S, DA_WIDTH), F32),
        scratch_shapes=[pltpu.VMEM((2, 2, tk, tq), F32), pltpu.VMEM((2, DA_V_DIM, tq), F32),
                        pltpu.VMEM((2, 1, tq), F32)],
        compiler_params=pltpu.CompilerParams(
            dimension_semantics=("parallel", "parallel", "arbitrary"), vmem_limit_bytes=VMEM_LIMIT),
        name="diff_attn",
    )(qt, qt, k, vt, lam_p, subln_g)


def _rwkv_prep(d, z_ref, pv_ref, wup_ref, aup_ref, gup_ref, seg):
    zs = z_ref[0]

    w = RW_WIDTH
    r, k, v = zs[:, 0:w], zs[:, w:2 * w], zs[:, 2 * w:3 * w]
    wa = zs[:, 3 * w:3 * w + LORA_IN]
    pv = lambda i: pv_ref[i:i + 1]
    w_pre = _dot(jnp.tanh(wa).astype(BF16), wup_ref[d])
    a_pre = _dot(wa.astype(BF16), aup_ref[d])
    lw = -math.exp(-0.5) * jax.nn.sigmoid(pv(PV_W0 + d) + w_pre)
    a_rate = jax.nn.sigmoid(pv(PV_A0 + d) + a_pre)
    kk = k * pv(PV_KK)
    kk = kk * lax.rsqrt(_seg_sum(kk * kk, seg) + KK_EPS)
    kd = k * (1.0 + (a_rate - 1.0) * pv(PV_KA))
    bonus = _seg_sum(r * kd * pv(PV_RK), seg) * v
    out = dict(r=r, v=v, kd=kd, lw=lw, a=-kk, b=kk * a_rate, bonus=bonus)
    if d == 0:
        g_dn = zs[:, 3 * w + LORA_IN:RW_COLS]
        out["g"] = _dot(jax.nn.sigmoid(g_dn).astype(BF16), gup_ref[...])
    return out


def _rwkv_chunks(cis, ops_ref, tri_ref, seg_ref, st_ref, y_refs):
    gw = SEG_WIDTH
    n_groups = RW_HEADS // GROUP_HEADS
    bmask = seg_ref[...]
    ti = lax.broadcasted_iota(jnp.int32, (CHUNK, gw), 0)
    si = lax.broadcasted_iota(jnp.int32, (CHUNK, gw), 1) % HEAD_DIM
    eye = (si == ti).astype(F32)
    strict = ((si < ti), (si > ti))
    incl = ((si <= ti), (si >= ti))
    nt = (((1,), (1,)), ((), ()))
    tn = (((0,), (0,)), ((), ()))
    lane = lax.broadcasted_iota(jnp.int32, (1, LANES), 1)
    half_masks = ((lane < HEAD_DIM).astype(BF16), (lane >= HEAD_DIM).astype(BF16))
    zeros = jnp.zeros((CHUNK, LANES), BF16)

    def blockdiag(x):
        xb = x.astype(BF16)
        blocks = []
        for h in range(GROUP_HEADS):
            part = xb[:, (h // 2) * LANES:(h // 2 + 1) * LANES] * half_masks[h % 2]
            blocks.append(jnp.concatenate([part, zeros] if h < 2 else [zeros, part], axis=1))
        return jnp.concatenate(blocks, axis=0)

    def stack(*xs):
        return jnp.concatenate([x.astype(BF16) for x in xs], axis=0)

    def blockdiag_t(x):
        tiled = jnp.concatenate([x.astype(F32)] * GROUP_HEADS, axis=0)
        return (tiled.T * bmask).astype(BF16)

    n_seq = len(cis[0])
    rows = [[pl.ds(pl.multiple_of(ci * CHUNK, CHUNK), CHUNK) for ci in cis[d]] for d in range(2)]
    state = {(d, g): st_ref[d, g] for d in range(2) for g in range(n_groups)}
    tails_done = [0]

    def position(k):
        chains = []
        for d in range(2):
            r, v, kd, lw, a, b = (ops_ref[d, n, rows[d][k], :] for n in range(6))
            cl = sum(_dot(tri_ref[d], part) for part in _split2(lw))
            tot = cl[CHUNK - 1:CHUNK] if d == 0 else cl[0:1]
            e_inv = jnp.exp(-cl)
            e_rem = jnp.exp(tot - cl)
            wide = dict(at=a * jnp.exp(cl - lw), rt=r * jnp.exp(cl), bt=b * e_inv, kt=kd * e_inv,
                        bh=b * e_rem, kh=kd * e_rem, v=v)
            wide = {name: x.astype(BF16) for name, x in wide.items()}
            wide["decay"] = jnp.exp(tot)
            for g in range(n_groups):
                c = {name: x[:, g * gw:(g + 1) * gw] for name, x in wide.items()}
                c.update(d=d, g=g)
                chains.append(c)
        yield

        for c in chains:
            ar = stack(c["at"], c["rt"])
            c["ab"] = _dot(ar, blockdiag_t(c["bt"]))
            c["ak"] = _dot(ar, blockdiag_t(c["kt"]))
        yield
        for c in chains:
            d = c["d"]
            c["l"] = jnp.where(strict[d], c["ab"][:CHUNK], 0.0)
            c["a_rb"] = jnp.where(incl[d], c["ab"][CHUNK:], 0.0).astype(BF16)
            a_ak = jnp.where(strict[d], c["ak"][:CHUNK], 0.0)
            a_rk = jnp.where(incl[d], c["ak"][CHUNK:], 0.0)
            av = _dot(stack(a_ak, a_rk), blockdiag(c["v"]))
            c["akv"], c["arkv"] = av[:CHUNK].astype(BF16), av[CHUNK:]
        yield

        for c in chains:
            c["t"] = eye + c["l"]
            c["lp"] = _dot(c["l"].astype(BF16), blockdiag(c["l"]))
        yield
        for _ in range(int(math.log2(CHUNK)) - 2):
            for c in chains:
                both = _dot(stack(c["t"], c["lp"]), blockdiag(c["lp"]))
                c["t"] = c["t"] + both[:CHUNK]
                c["lp"] = both[CHUNK:]
            yield
        for c in chains:
            c["t"] = (c["t"] + _dot(c["t"].astype(BF16), blockdiag(c["lp"]))).astype(BF16)
        yield
        for c in chains:
            c["wt"] = _dot(c["t"], blockdiag(c["at"])).astype(BF16)
            c["u_loc"] = _dot(c["t"], blockdiag(c["akv"]))
        yield "tail"

        assert tails_done[0] == k
        for c in chains:
            wr = _dot(stack(c["wt"], c["rt"]), blockdiag(state[c["d"], c["g"]]))
            c["u"] = wr[:CHUNK] + c["u_loc"]
            c["y"] = wr[CHUNK:] + c["arkv"]
        yield
        for c in chains:
            c["y"] = c["y"] + _dot(c["a_rb"], blockdiag(c["u"]))
            st_hi, st_lo = _split2(state[c["d"], c["g"]])
            decay = eye * c["decay"]
            full = lax.dot_general(stack(c["bh"], c["kh"], decay, decay), stack(c["u"], c["v"], st_hi, st_lo),
                                   tn, preferred_element_type=F32) * bmask
            state[c["d"], c["g"]] = sum(full[h * HEAD_DIM:(h + 1) * HEAD_DIM] for h in range(GROUP_HEADS))
        tails_done[0] = k + 1
        for d in range(2):
            y_refs[d][0, rows[d][k], :] = jnp.concatenate([c["y"] for c in chains if c["d"] == d], axis=1)

    positions = [position(k) for k in range(n_seq)]
    while all([next(p) != "tail" for p in positions]):
        pass
    for p in positions:
        for _ in p:
            pass
    for (d, g), st in state.items():
        st_ref[d, g] = st


def _rwkv_kernel(zf_ref, zb_ref, pv_ref, wup_ref, aup_ref, gup_ref, seg_ref, tri_ref,
                 y0_ref, y1_ref, bon0_ref, bon1_ref, g_ref, ops_ref, st_ref):
    tm = zf_ref.shape[1]

    @pl.when(pl.program_id(1) == 0)
    def _():
        st_ref[...] = jnp.zeros_like(st_ref)

    seg = seg_ref[...]
    for d, (z_ref, bon_ref) in enumerate(((zf_ref, bon0_ref), (zb_ref, bon1_ref))):
        p = _rwkv_prep(d, z_ref, pv_ref, wup_ref, aup_ref, gup_ref, seg)
        for n, name in enumerate(("r", "v", "kd", "lw", "a", "b")):
            ops_ref[d, n] = p[name]
        bon_ref[0] = p["bonus"]
        if d == 0:
            g_ref[0] = p["g"]

    n_chunks = tm // CHUNK
    per_iter = min(CHUNKS_PER_ITER, n_chunks)

    def body(it, carry):
        fwd = [it * per_iter + k for k in range(per_iter)]
        bwd = [n_chunks - 1 - c for c in fwd]
        _rwkv_chunks((fwd, bwd), ops_ref, tri_ref, seg_ref, st_ref, (y0_ref, y1_ref))
        return carry

    lax.fori_loop(0, n_chunks // per_iter, body, 0)


def _rwkv_scan(z, pvec, wup, aup, gup, seg, tri, tm):
    B, S, _ = z.shape
    nt = S // tm
    const = lambda shape: pl.BlockSpec(shape, lambda b, i: (0,) * len(shape))
    fwd = lambda width: pl.BlockSpec((1, tm, width), lambda b, i: (b, i, 0))
    bwd = lambda width: pl.BlockSpec((1, tm, width), lambda b, i: (b, nt - 1 - i, 0))
    tok = jax.ShapeDtypeStruct((B, S, RW_WIDTH), F32)
    return pl.pallas_call(
        _rwkv_kernel,
        grid=(B, nt),
        in_specs=[fwd(RW_COLS), bwd(RW_COLS), const((PV_ROWS, RW_WIDTH)),
                  const((2, LORA_IN, RW_WIDTH)), const((2, LORA_IN, RW_WIDTH)), const((GATE_LORA, RW_WIDTH)),
                  const((SEG_WIDTH, SEG_WIDTH)), const((2, CHUNK, CHUNK))],
        out_specs=[fwd(RW_WIDTH), bwd(RW_WIDTH), fwd(RW_WIDTH), bwd(RW_WIDTH), fwd(RW_WIDTH)],
        out_shape=[tok] * 5,
        scratch_shapes=[pltpu.VMEM((2, 6, tm, RW_WIDTH), F32),
                        pltpu.VMEM((2, RW_HEADS // GROUP_HEADS, HEAD_DIM, GROUP_HEADS * HEAD_DIM), F32)],
        compiler_params=pltpu.CompilerParams(
            dimension_semantics=("arbitrary", "arbitrary"), vmem_limit_bytes=VMEM_LIMIT),
        name="rwkv_scan",
    )(z, z, pvec, wup, aup, gup, seg, tri)


def _out_ffn_kernel(x_ref, oda_ref, y0_ref, y1_ref, bon0_ref, bon1_ref, g_ref, pv_ref, seg_ref,
                    wout_ref, g2_ref, w1_ref, w2_ref, o_ref):
    seg = seg_ref[...]
    y = y0_ref[0] + y1_ref[0]
    mean = _seg_sum(y, seg) * (1.0 / HEAD_DIM)
    yc = y - mean
    var = _seg_sum(yc * yc, seg) * (1.0 / HEAD_DIM)
    yn = yc * lax.rsqrt(var + LN_X_EPS) * pv_ref[PV_LNG:PV_LNG + 1] + pv_ref[PV_LNB:PV_LNB + 1]
    o_rw = (yn + bon0_ref[0] + bon1_ref[0]) * g_ref[0]
    x = (x_ref[0] + _dot(oda_ref[0].astype(BF16), wout_ref[0:DA_WIDTH, :])
         + _dot(o_rw.astype(BF16), wout_ref[DA_WIDTH:D_MODEL, :]))
    ms = jnp.mean(x * x, axis=-1, keepdims=True)
    h = (x * lax.rsqrt(ms + NORM_EPS) * g2_ref[...]).astype(BF16)
    ffn = None
    for c0 in range(0, D_FF, FF_CHUNK):
        u = jnp.maximum(_dot(h, w1_ref[:, c0:c0 + FF_CHUNK]), 0.0)
        part = _dot((u * u).astype(BF16), w2_ref[c0:c0 + FF_CHUNK, :])
        ffn = part if ffn is None else ffn + part
    o_ref[0] = x + ffn


def _out_ffn(x, o_da, y0, y1, bon0, bon1, g, pvec, seg, w_out, g2, w1, w2, tm):
    B, S, _ = x.shape
    const = lambda shape: pl.BlockSpec(shape, lambda b, i: (0,) * len(shape))
    tok = lambda width: pl.BlockSpec((1, tm, width), lambda b, i: (b, i, 0))
    return pl.pallas_call(
        _out_ffn_kernel,
        grid=(B, S // tm),
        in_specs=[tok(D_MODEL)] + [tok(RW_WIDTH)] * 6 + [
            const((PV_ROWS, RW_WIDTH)), const((SEG_WIDTH, SEG_WIDTH)), const((D_MODEL, D_MODEL)),
            const((1, D_MODEL)), const((D_MODEL, D_FF)), const((D_FF, D_MODEL))],
        out_specs=tok(D_MODEL),
        out_shape=jax.ShapeDtypeStruct((B, S, D_MODEL), F32),
        compiler_params=pltpu.CompilerParams(
            dimension_semantics=("parallel", "parallel"), vmem_limit_bytes=VMEM_LIMIT),
        name="out_ffn",
    )(x, o_da, y0, y1, bon0, bon1, g, pvec, seg, w_out, g2, w1, w2)


def _rope_tables(seq_len):
    inv_freq = 1.0 / (ROPE_THETA ** (jnp.arange(0, HEAD_DIM, 2, dtype=F32) / HEAD_DIM))
    half = HEAD_DIM // 2
    inv_lanes = jnp.tile(inv_freq, LANES // half)
    sign = jnp.tile(jnp.concatenate([-jnp.ones(half, F32), jnp.ones(half, F32)]), LANES // HEAD_DIM)
    ang = jnp.arange(seq_len, dtype=F32)[:, None] * inv_lanes[None, :]
    return jnp.cos(ang), jnp.sin(ang) * sign


def _attn_tiles(seq_len):
    tq = 256 if seq_len >= 8192 else _pick(seq_len, 512)
    tk = _pick(seq_len // 2, 512)
    n_blocks = seq_len // tk
    return tq, tk, n_blocks if n_blocks % 2 == 0 and n_blocks <= 32 else 2


def _pick(n, target):
    t = min(n, target)
    assert n % t == 0, (n, t)
    return t


def kernel(x_prompt, x_sample, norm1_g, w_in, q_norm_g, k_norm_g, lam_q1, lam_k1, lam_q2, lam_k2, subln_g,
           mu_prev, mu_next, w0, w_up, a0, a_up, g_up, k_k, k_a, r_k, ln_x_g, ln_x_b, w_out, norm2_g,
           w_ff1, w_ff2):
    l = 0
    w_in_b = w_in[l].astype(BF16)
    w_out_b = w_out[l].astype(BF16)
    w1_b = w_ff1[l].astype(BF16)
    w2_b = w_ff2[l].astype(BF16)
    g1 = norm1_g[l][None, :]
    g2 = norm2_g[l][None, :]
    qg = jnp.tile(q_norm_g[l], DA_WIDTH // HEAD_DIM)[None, :]
    kg = jnp.tile(k_norm_g[l], DA_WIDTH // HEAD_DIM)[None, :]
    lam_p = jnp.stack([lam_q1[l], lam_k1[l], lam_q2[l], lam_k2[l]])
    sg = subln_g[l][None, :]
    mu = jnp.stack([mu_prev[l], mu_next[l]])
    rows = [w0[l, 0], w0[l, 1], a0[l, 0], a0[l, 1], k_k[l], k_a[l], r_k[l].reshape(-1), ln_x_g[l], ln_x_b[l]]
    pvec = jnp.zeros((PV_ROWS, RW_WIDTH), F32).at[:len(rows)].set(jnp.stack(rows))
    zpad = jnp.zeros((2, DECAY_LORA, RW_WIDTH), F32)
    wup = jnp.concatenate([w_up[l], zpad], axis=1).astype(BF16)
    aup = jnp.concatenate([zpad, a_up[l]], axis=1).astype(BF16)
    gup = g_up[l].astype(BF16)
    ch = jnp.arange(SEG_WIDTH) // HEAD_DIM
    seg = (ch[:, None] == ch[None, :]).astype(BF16)
    t = jnp.arange(CHUNK)
    tri = jnp.stack([t[None, :] <= t[:, None], t[None, :] >= t[:, None]]).astype(BF16)

    cos_t, sin_t = _rope_tables(max(x_prompt.shape[1], x_sample.shape[1]))

    def run(x):
        S = x.shape[1]
        tm = _pick(S, TOKEN_TILE)
        qt, k, vt, z_rw = _in_proj(x, g1, w_in_b, qg, kg, cos_t, sin_t, seg, mu, tm)
        tq, tk, per_iter = _attn_tiles(S)
        o_da = _diff_attn(qt, k, vt, lam_p, sg, tq, tk, per_iter)
        y0, y1, bon0, bon1, g = _rwkv_scan(z_rw, pvec, wup, aup, gup, seg, tri, _pick(S, SCAN_TILE))
        return _out_ffn(x, o_da, y0, y1, bon0, bon1, g, pvec, seg, w_out_b, g2, w1_b, w2_b, tm)

    return (run(x_prompt), run(x_sample))
```

```python
import functools
import math

import jax
import jax.numpy as jnp
from jax import lax
from jax.experimental import pallas as pl
from jax.experimental.pallas import tpu as pltpu

F32 = jnp.float32
BF16 = jnp.bfloat16

D_MODEL = 1024
DA_HEADS = 4
HEAD_DIM = 64
DA_V_DIM = 128
DA_WIDTH = DA_HEADS * DA_V_DIM
RW_WIDTH = D_MODEL - DA_WIDTH
RW_HEADS = RW_WIDTH // HEAD_DIM
DECAY_LORA = 64
ICLR_LORA = 64
GATE_LORA = 128
LORA_IN = DECAY_LORA + ICLR_LORA
RW_COLS = 3 * RW_WIDTH + LORA_IN + GATE_LORA
DA_COLS = 3 * DA_WIDTH
IN_COLS = DA_COLS + RW_COLS
D_FF = 4 * D_MODEL
ROPE_THETA = 10000.0
NORM_EPS = 1e-6
LN_X_EPS = 64e-5
KK_EPS = 1e-12
LAMBDA_INIT = 0.8 - 0.6 * math.exp(-0.3 * 0)
LOG2E = 1.4426950408889634
QK_SCALE = HEAD_DIM ** -0.5

LANES = 128
BF16_SUBLANES = 16
CHUNK = 64
GROUP_HEADS = 4
SEG_WIDTH = GROUP_HEADS * HEAD_DIM
CHUNKS_PER_ITER = 4
SCAN_TILE = CHUNK * CHUNKS_PER_ITER
TOKEN_TILE = 512
ATTN_HEADS_PER_STEP = 2
FF_CHUNK = 1024
VMEM_LIMIT = 56 * 1024 * 1024

PV_W0, PV_A0, PV_KK, PV_KA, PV_RK, PV_LNG, PV_LNB = 0, 2, 4, 5, 6, 7, 8
PV_ROWS = 16


def _dot(a, b):
    return jnp.dot(a, b, preferred_element_type=F32)


def _split2(x):
    hi = x.astype(BF16)
    lo = (x - hi.astype(F32)).astype(BF16)
    return hi, lo


def _seg_sum(x, seg):
    xb = x.astype(BF16)
    gw = seg.shape[0]
    return jnp.concatenate([_dot(xb[:, g * gw:(g + 1) * gw], seg) for g in range(x.shape[1] // gw)], axis=1)


def _in_proj_kernel(x_ref, xp_ref, xn_ref, g1_ref, w_ref, qg_ref, kg_ref, cos_ref, sin_ref, seg_ref, mu_ref,
                    qt_ref, k_ref, vt_ref, z_ref):
    def norm1(x):
        ms = jnp.mean(x * x, axis=-1, keepdims=True)
        return (x * lax.rsqrt(ms + NORM_EPS) * g1_ref[...]).astype(BF16)

    h = norm1(x_ref[0])
    tm = h.shape[0]
    seg = seg_ref[...]
    reps = DA_WIDTH // LANES
    cos = jnp.concatenate([cos_ref[...]] * reps, axis=1)
    sin = jnp.concatenate([sin_ref[...]] * reps, axis=1)
    lane = lax.broadcasted_iota(jnp.int32, (1, LANES), 1)
    first_half = (lane % HEAD_DIM) < (HEAD_DIM // 2)

    def head_norm_rope(z, g):
        ss = _seg_sum(z * z, seg) * (1.0 / HEAD_DIM)
        zn = z * lax.rsqrt(ss + NORM_EPS) * g
        parts = []
        for c in range(reps):
            zc = zn[:, c * LANES:(c + 1) * LANES]
            parts.append(jnp.where(first_half,
                                   pltpu.roll(zc, LANES - HEAD_DIM // 2, 1),
                                   pltpu.roll(zc, HEAD_DIM // 2, 1)))
        rot = jnp.concatenate(parts, axis=1)
        return zn * cos + rot * sin

    zq = _dot(h, w_ref[:, 0:DA_WIDTH])
    qt_ref[0] = (head_norm_rope(zq, qg_ref[...]) * (QK_SCALE * LOG2E)).T.astype(BF16)
    zk = _dot(h, w_ref[:, DA_WIDTH:2 * DA_WIDTH])
    k_ref[0] = head_norm_rope(zk, kg_ref[...]).astype(BF16)
    vt_ref[0] = _dot(h, w_ref[:, 2 * DA_WIDTH:DA_COLS]).T.astype(BF16)

    i = pl.program_id(1)
    h_ext = jnp.concatenate([h, norm1(xp_ref[0]), norm1(xn_ref[0])], axis=0)
    row = lax.broadcasted_iota(jnp.int32, (tm, 1), 0)
    slab = lambda c0: _dot(h_ext, w_ref[:, DA_COLS + c0:DA_COLS + c0 + SEG_WIDTH])
    z_next_group = slab(0)
    for c0 in range(0, RW_COLS, SEG_WIDTH):
        cols = slice(c0, c0 + SEG_WIDTH)
        z_ext = z_next_group
        if c0 + SEG_WIDTH < RW_COLS:
            z_next_group = slab(c0 + SEG_WIDTH)
        z = z_ext[:tm]
        prev_row = jnp.where(i > 0, z_ext[tm + 7:tm + 8], 0.0)
        next_row = jnp.where(i < pl.num_programs(1) - 1, z_ext[tm + 8:tm + 9], 0.0)
        z_prev = jnp.where(row == 0, prev_row, pltpu.roll(z, 1, 0))
        z_next = jnp.where(row == tm - 1, next_row, pltpu.roll(z, tm - 1, 0))
        z_ref[0, :, cols] = z + mu_ref[0:1, cols] * (z_prev - z) + mu_ref[1:2, cols] * (z_next - z)


def _in_proj(x, g1, w_in, qg, kg, cos_t, sin_t, seg, mu, tm):
    B, S, _ = x.shape
    const = lambda shape: pl.BlockSpec(shape, lambda b, i: (0,) * len(shape))
    tok = lambda width: pl.BlockSpec((1, tm, width), lambda b, i: (b, i, 0))
    tok_t = pl.BlockSpec((1, DA_WIDTH, tm), lambda b, i: (b, 0, i))
    halo_prev = pl.BlockSpec((1, 8, D_MODEL), lambda b, i: (b, jnp.maximum(i * (tm // 8) - 1, 0), 0))
    halo_next = pl.BlockSpec((1, 8, D_MODEL), lambda b, i: (b, jnp.minimum((i + 1) * (tm // 8), S // 8 - 1), 0))
    return pl.pallas_call(
        _in_proj_kernel,
        grid=(B, S // tm),
        in_specs=[tok(D_MODEL), halo_prev, halo_next, const((1, D_MODEL)), const((D_MODEL, IN_COLS)),
                  const((1, DA_WIDTH)), const((1, DA_WIDTH)),
                  pl.BlockSpec((tm, LANES), lambda b, i: (i, 0)),
                  pl.BlockSpec((tm, LANES), lambda b, i: (i, 0)),
                  const((SEG_WIDTH, SEG_WIDTH)), const((2, RW_COLS))],
        out_specs=[tok_t, tok(DA_WIDTH), tok_t, tok(RW_COLS)],
        out_shape=[jax.ShapeDtypeStruct((B, DA_WIDTH, S), BF16), jax.ShapeDtypeStruct((B, S, DA_WIDTH), BF16),
                   jax.ShapeDtypeStruct((B, DA_WIDTH, S), BF16), jax.ShapeDtypeStruct((B, S, RW_COLS), F32)],
        compiler_params=pltpu.CompilerParams(
            dimension_semantics=("parallel", "parallel"), vmem_limit_bytes=VMEM_LIMIT),
        name="in_proj",
    )(x, x, x, g1, w_in, qg, kg, cos_t, sin_t, seg, mu)


def _diff_attn_kernel(qt_ref, qtn_ref, k_ref, vt_ref, lam_ref, sg_ref, o_ref, s_ref, acc_ref, mb_ref, *, tk,
                      per_iter):
    tq = qt_ref.shape[2]
    nk = k_ref.shape[1] // tk
    n_streams = 2 * ATTN_HEADS_PER_STEP
    row = lax.broadcasted_iota(jnp.int32, (LANES, 1), 0)

    def components(qt):
        out = []
        for j in range(ATTN_HEADS_PER_STEP):
            q = qt[j * LANES:(j + 1) * LANES]
            zero = jnp.zeros_like(q)
            out += [jnp.where(row < HEAD_DIM, q, zero), jnp.where(row >= HEAD_DIM, q, zero)]
        return out

    qts = components(qt_ref[0])
    acc_ref[...] = jnp.zeros_like(acc_ref)

    def scores(slot, blk, q_pair):
        mblk = []
        for n in range(n_streams):
            j = n // 2
            kb = k_ref[0, pl.ds(pl.multiple_of(blk * tk, tk), tk), j * LANES:(j + 1) * LANES]
            s = _dot(kb, q_pair[n])
            s_ref[slot, n] = s
            mblk.append(jnp.max(s, axis=0, keepdims=True))
        return tuple(mblk)

    @pl.when(pl.program_id(2) == 0)
    def _():
        first = scores(0, 0, qts)
        for n in range(n_streams):
            mb_ref[n] = first[n]

    def consume(slot, blk, mblk, ml):
        out = []
        for n in range(n_streams):
            j = n // 2
            vtb = vt_ref[0, j * LANES:(j + 1) * LANES, pl.ds(pl.multiple_of(blk * tk, tk), tk)]
            vtb_ones = jnp.concatenate([vtb, jnp.ones((BF16_SUBLANES, tk), BF16)], axis=0)
            m, l = ml[n]
            m_new = jnp.maximum(m, mblk[n])
            alpha = jnp.exp2(m - m_new)
            p = jnp.exp2(s_ref[slot, n] - m_new)
            pv = _dot(vtb_ones, p.astype(BF16))
            l = alpha * l + pv[DA_V_DIM:DA_V_DIM + 1]
            acc_ref[n] = alpha * acc_ref[n] + pv[:DA_V_DIM]
            out.append((m_new, l))
        return tuple(out)

    def steps(first, carry, last):
        mblk, ml = carry
        for u in range(per_iter):
            if last and u == per_iter - 1:
                nxt = scores(0, 0, components(qtn_ref[0]))
            else:
                nxt = scores((u + 1) % 2, first + u + 1, qts)
            ml = consume(u % 2, first + u, mblk, ml)
            mblk = nxt
        return mblk, ml

    ml = tuple((jnp.full((1, tq), -1e30, F32), jnp.zeros((1, tq), F32)) for _ in range(n_streams))
    carry = lax.fori_loop(0, nk // per_iter - 1, lambda i, c: steps(i * per_iter, c, False),
                          (tuple(mb_ref[n] for n in range(n_streams)), ml))
    mb_next, ml = steps(nk - per_iter, carry, True)
    for n in range(n_streams):
        mb_ref[n] = mb_next[n]

    lp = lam_ref[...]
    lam = (jnp.exp(jnp.sum(lp[0:1] * lp[1:2], axis=1, keepdims=True))
           - jnp.exp(jnp.sum(lp[2:3] * lp[3:4], axis=1, keepdims=True)) + LAMBDA_INIT)
    for j in range(ATTN_HEADS_PER_STEP):
        (_, l0), (_, l1) = ml[2 * j], ml[2 * j + 1]
        o = (acc_ref[2 * j] / l0 - lam * (acc_ref[2 * j + 1] / l1)).T
        ms = jnp.mean(o * o, axis=-1, keepdims=True)
        o_ref[0, :, j * DA_V_DIM:(j + 1) * DA_V_DIM] = (o * lax.rsqrt(ms + NORM_EPS) * sg_ref[...]
                                                        * (1.0 - LAMBDA_INIT))


def _diff_attn(qt, k, vt, lam_p, subln_g, tq, tk, per_iter):
    B, S, _ = k.shape
    assert per_iter % 2 == 0 and (S // tk) % per_iter == 0
    nq = S // tq
    nh = ATTN_HEADS_PER_STEP
    return pl.pallas_call(
        functools.partial(_diff_attn_kernel, tk=tk, per_iter=per_iter),
        grid=(B, DA_HEADS // nh, nq),
        in_specs=[pl.BlockSpec((1, nh * LANES, tq), lambda b, h, i: (b, h, i)),
                  pl.BlockSpec((1, nh * LANES, tq), lambda b, h, i: (b, h, jnp.minimum(i + 1, nq - 1))),
                  pl.BlockSpec((1, S, nh * LANES), lambda b, h, i: (b, 0, h)),
                  pl.BlockSpec((1, nh * LANES, S), lambda b, h, i: (b, h, 0)),
                  pl.BlockSpec((4, HEAD_DIM), lambda b, h, i: (0, 0)),
                  pl.BlockSpec((1, DA_V_DIM), lambda b, h, i: (0, 0))],
        out_specs=pl.BlockSpec((1, tq, nh * DA_V_DIM), lambda b, h, i: (b, i, h)),
        out_shape=jax.ShapeDtypeStruct((B, S, DA_WIDTH), F32),
        scratch_shapes=[pltpu.VMEM((2, 2 * nh, tk, tq), F32), pltpu.VMEM((2 * nh, DA_V_DIM, tq), F32),
                        pltpu.VMEM((2 * nh, 1, tq), F32)],
        compiler_params=pltpu.CompilerParams(
            dimension_semantics=("parallel", "parallel", "arbitrary"), vmem_limit_bytes=VMEM_LIMIT),
        name="diff_attn",
    )(qt, qt, k, vt, lam_p, subln_g)


def _rwkv_prep(d, z_ref, pv_ref, wup_ref, aup_ref, gup_ref, seg):
    zs = z_ref[0]

    w = RW_WIDTH
    r, k, v = zs[:, 0:w], zs[:, w:2 * w], zs[:, 2 * w:3 * w]
    wa = zs[:, 3 * w:3 * w + LORA_IN]
    pv = lambda i: pv_ref[i:i + 1]
    w_pre = _dot(jnp.tanh(wa).astype(BF16), wup_ref[d])
    a_pre = _dot(wa.astype(BF16), aup_ref[d])
    lw = -math.exp(-0.5) * jax.nn.sigmoid(pv(PV_W0 + d) + w_pre)
    a_rate = jax.nn.sigmoid(pv(PV_A0 + d) + a_pre)
    kk = k * pv(PV_KK)
    kk = kk * lax.rsqrt(_seg_sum(kk * kk, seg) + KK_EPS)
    kd = k * (1.0 + (a_rate - 1.0) * pv(PV_KA))
    bonus = _seg_sum(r * kd * pv(PV_RK), seg) * v
    out = dict(r=r, v=v, kd=kd, lw=lw, a=-kk, b=kk * a_rate, bonus=bonus)
    if d == 0:
        g_dn = zs[:, 3 * w + LORA_IN:RW_COLS]
        out["g"] = _dot(jax.nn.sigmoid(g_dn).astype(BF16), gup_ref[...])
    return out


def _rwkv_chunks(cis, ops_ref, tri_ref, seg_ref, st_ref, y_refs):
    gw = SEG_WIDTH
    n_groups = RW_HEADS // GROUP_HEADS
    bmask = seg_ref[...]
    ti = lax.broadcasted_iota(jnp.int32, (CHUNK, gw), 0)
    si = lax.broadcasted_iota(jnp.int32, (CHUNK, gw), 1) % HEAD_DIM
    eye = (si == ti).astype(F32)
    strict = ((si < ti), (si > ti))
    incl = ((si <= ti), (si >= ti))
    nt = (((1,), (1,)), ((), ()))
    tn = (((0,), (0,)), ((), ()))
    lane = lax.broadcasted_iota(jnp.int32, (1, LANES), 1)
    half_masks = ((lane < HEAD_DIM).astype(BF16), (lane >= HEAD_DIM).astype(BF16))
    zeros = jnp.zeros((CHUNK, LANES), BF16)

    def blockdiag(x):
        xb = x.astype(BF16)
        blocks = []
        for h in range(GROUP_HEADS):
            part = xb[:, (h // 2) * LANES:(h // 2 + 1) * LANES] * half_masks[h % 2]
            blocks.append(jnp.concatenate([part, zeros] if h < 2 else [zeros, part], axis=1))
        return jnp.concatenate(blocks, axis=0)

    def stack(*xs):
        return jnp.concatenate([x.astype(BF16) for x in xs], axis=0)

    n_seq = len(cis[0])
    rows = [[pl.ds(pl.multiple_of(ci * CHUNK, CHUNK), CHUNK) for ci in cis[d]] for d in range(2)]
    state = {(d, g): st_ref[d, g] for d in range(2) for g in range(n_groups)}
    tails_done = [0]

    def position(k):
        chains = []
        for d in range(2):
            r, v, kd, lw, a, b = (ops_ref[d, n, rows[d][k], :] for n in range(6))
            cl = sum(_dot(tri_ref[d], part) for part in _split2(lw))
            tot = cl[CHUNK - 1:CHUNK] if d == 0 else cl[0:1]
            e_inv = jnp.exp(-cl)
            e_rem = jnp.exp(tot - cl)
            wide = dict(at=a * jnp.exp(cl - lw), rt=r * jnp.exp(cl), bt=b * e_inv, kt=kd * e_inv,
                        bh=b * e_rem, kh=kd * e_rem, v=v)
            wide = {name: x.astype(BF16) for name, x in wide.items()}
            wide["decay"] = jnp.exp(tot)
            for g in range(n_groups):
                c = {name: x[:, g * gw:(g + 1) * gw] for name, x in wide.items()}
                c.update(d=d, g=g)
                chains.append(c)
        yield

        for c in chains:
            ar = stack(c["at"], c["rt"])
            c["ab"] = lax.dot_general(ar, blockdiag(c["bt"]), nt, preferred_element_type=F32)
            c["ak"] = lax.dot_general(ar, blockdiag(c["kt"]), nt, preferred_element_type=F32)
        yield
        for c in chains:
            d = c["d"]
            c["l"] = jnp.where(strict[d], c["ab"][:CHUNK], 0.0)
            c["a_rb"] = jnp.where(incl[d], c["ab"][CHUNK:], 0.0).astype(BF16)
            a_ak = jnp.where(strict[d], c["ak"][:CHUNK], 0.0)
            a_rk = jnp.where(incl[d], c["ak"][CHUNK:], 0.0)
            av = _dot(stack(a_ak, a_rk), blockdiag(c["v"]))
            c["akv"], c["arkv"] = av[:CHUNK].astype(BF16), av[CHUNK:]
        yield

        for c in chains:
            c["t"] = eye + c["l"]
            c["lp"] = _dot(c["l"].astype(BF16), blockdiag(c["l"]))
        yield
        for _ in range(int(math.log2(CHUNK)) - 2):
            for c in chains:
                both = _dot(stack(c["t"], c["lp"]), blockdiag(c["lp"]))
                c["t"] = c["t"] + both[:CHUNK]
                c["lp"] = both[CHUNK:]
            yield
        for c in chains:
            c["t"] = (c["t"] + _dot(c["t"].astype(BF16), blockdiag(c["lp"]))).astype(BF16)
        yield
        for c in chains:
            c["wt"] = _dot(c["t"], blockdiag(c["at"])).astype(BF16)
            c["u_loc"] = _dot(c["t"], blockdiag(c["akv"]))
        yield "tail"

        assert tails_done[0] == k
        for c in chains:
            wr = _dot(stack(c["wt"], c["rt"]), blockdiag(state[c["d"], c["g"]]))
            c["u"] = wr[:CHUNK] + c["u_loc"]
            c["y"] = wr[CHUNK:] + c["arkv"]
        yield
        for c in chains:
            c["y"] = c["y"] + _dot(c["a_rb"], blockdiag(c["u"]))
            st_hi, st_lo = _split2(state[c["d"], c["g"]])
            decay = eye * c["decay"]
            full = lax.dot_general(stack(c["bh"], c["kh"], decay, decay), stack(c["u"], c["v"], st_hi, st_lo),
                                   tn, preferred_element_type=F32) * bmask
            state[c["d"], c["g"]] = sum(full[h * HEAD_DIM:(h + 1) * HEAD_DIM] for h in range(GROUP_HEADS))
        tails_done[0] = k + 1
        for d in range(2):
            y_refs[d][0, rows[d][k], :] = jnp.concatenate([c["y"] for c in chains if c["d"] == d], axis=1)

    positions = [position(k) for k in range(n_seq)]
    while all([next(p) != "tail" for p in positions]):
        pass
    for p in positions:
        for _ in p:
            pass
    for (d, g), st in state.items():
        st_ref[d, g] = st


def _rwkv_kernel(zf_ref, zb_ref, pv_ref, wup_ref, aup_ref, gup_ref, seg_ref, tri_ref,
                 y0_ref, y1_ref, bon0_ref, bon1_ref, g_ref, ops_ref, st_ref):
    tm = zf_ref.shape[1]

    @pl.when(pl.program_id(1) == 0)
    def _():
        st_ref[...] = jnp.zeros_like(st_ref)

    seg = seg_ref[...]
    for d, (z_ref, bon_ref) in enumerate(((zf_ref, bon0_ref), (zb_ref, bon1_ref))):
        p = _rwkv_prep(d, z_ref, pv_ref, wup_ref, aup_ref, gup_ref, seg)
        for n, name in enumerate(("r", "v", "kd", "lw", "a", "b")):
            ops_ref[d, n] = p[name]
        bon_ref[0] = p["bonus"]
        if d == 0:
            g_ref[0] = p["g"]

    n_chunks = tm // CHUNK
    per_iter = min(CHUNKS_PER_ITER, n_chunks)

    def body(it, carry):
        fwd = [it * per_iter + k for k in range(per_iter)]
        bwd = [n_chunks - 1 - c for c in fwd]
        _rwkv_chunks((fwd, bwd), ops_ref, tri_ref, seg_ref, st_ref, (y0_ref, y1_ref))
        return carry

    lax.fori_loop(0, n_chunks // per_iter, body, 0)


def _rwkv_scan(z, pvec, wup, aup, gup, seg, tri, tm):
    B, S, _ = z.shape
    nt = S // tm
    const = lambda shape: pl.BlockSpec(shape, lambda b, i: (0,) * len(shape))
    fwd = lambda width: pl.BlockSpec((1, tm, width), lambda b, i: (b, i, 0))
    bwd = lambda width: pl.BlockSpec((1, tm, width), lambda b, i: (b, nt - 1 - i, 0))
    tok = jax.ShapeDtypeStruct((B, S, RW_WIDTH), F32)
    return pl.pallas_call(
        _rwkv_kernel,
        grid=(B, nt),
        in_specs=[fwd(RW_COLS), bwd(RW_COLS), const((PV_ROWS, RW_WIDTH)),
                  const((2, LORA_IN, RW_WIDTH)), const((2, LORA_IN, RW_WIDTH)), const((GATE_LORA, RW_WIDTH)),
                  const((SEG_WIDTH, SEG_WIDTH)), const((2, CHUNK, CHUNK))],
        out_specs=[fwd(RW_WIDTH), bwd(RW_WIDTH), fwd(RW_WIDTH), bwd(RW_WIDTH), fwd(RW_WIDTH)],
        out_shape=[tok] * 5,
        scratch_shapes=[pltpu.VMEM((2, 6, tm, RW_WIDTH), F32),
                        pltpu.VMEM((2, RW_HEADS // GROUP_HEADS, HEAD_DIM, GROUP_HEADS * HEAD_DIM), F32)],
        compiler_params=pltpu.CompilerParams(
            dimension_semantics=("arbitrary", "arbitrary"), vmem_limit_bytes=VMEM_LIMIT),
        name="rwkv_scan",
    )(z, z, pvec, wup, aup, gup, seg, tri)


def _out_ffn_kernel(x_ref, oda_ref, y0_ref, y1_ref, bon0_ref, bon1_ref, g_ref, pv_ref, seg_ref,
                    wout_ref, g2_ref, w1_ref, w2_ref, o_ref):
    seg = seg_ref[...]
    y = y0_ref[0] + y1_ref[0]
    mean = _seg_sum(y, seg) * (1.0 / HEAD_DIM)
    yc = y - mean
    var = _seg_sum(yc * yc, seg) * (1.0 / HEAD_DIM)
    yn = yc * lax.rsqrt(var + LN_X_EPS) * pv_ref[PV_LNG:PV_LNG + 1] + pv_ref[PV_LNB:PV_LNB + 1]
    o_rw = (yn + bon0_ref[0] + bon1_ref[0]) * g_ref[0]
    x = (x_ref[0] + _dot(oda_ref[0].astype(BF16), wout_ref[0:DA_WIDTH, :])
         + _dot(o_rw.astype(BF16), wout_ref[DA_WIDTH:D_MODEL, :]))
    ms = jnp.mean(x * x, axis=-1, keepdims=True)
    h = (x * lax.rsqrt(ms + NORM_EPS) * g2_ref[...]).astype(BF16)
    ffn = None
    for c0 in range(0, D_FF, FF_CHUNK):
        u = jnp.maximum(_dot(h, w1_ref[:, c0:c0 + FF_CHUNK]), 0.0)
        part = _dot((u * u).astype(BF16), w2_ref[c0:c0 + FF_CHUNK, :])
        ffn = part if ffn is None else ffn + part
    o_ref[0] = x + ffn


def _out_ffn(x, o_da, y0, y1, bon0, bon1, g, pvec, seg, w_out, g2, w1, w2, tm):
    B, S, _ = x.shape
    const = lambda shape: pl.BlockSpec(shape, lambda b, i: (0,) * len(shape))
    tok = lambda width: pl.BlockSpec((1, tm, width), lambda b, i: (b, i, 0))
    return pl.pallas_call(
        _out_ffn_kernel,
        grid=(B, S // tm),
        in_specs=[tok(D_MODEL)] + [tok(RW_WIDTH)] * 6 + [
            const((PV_ROWS, RW_WIDTH)), const((SEG_WIDTH, SEG_WIDTH)), const((D_MODEL, D_MODEL)),
            const((1, D_MODEL)), const((D_MODEL, D_FF)), const((D_FF, D_MODEL))],
        out_specs=tok(D_MODEL),
        out_shape=jax.ShapeDtypeStruct((B, S, D_MODEL), F32),
        compiler_params=pltpu.CompilerParams(
            dimension_semantics=("parallel", "parallel"), vmem_limit_bytes=VMEM_LIMIT),
        name="out_ffn",
    )(x, o_da, y0, y1, bon0, bon1, g, pvec, seg, w_out, g2, w1, w2)


def _rope_tables(seq_len):
    inv_freq = 1.0 / (ROPE_THETA ** (jnp.arange(0, HEAD_DIM, 2, dtype=F32) / HEAD_DIM))
    half = HEAD_DIM // 2
    inv_lanes = jnp.tile(inv_freq, LANES // half)
    sign = jnp.tile(jnp.concatenate([-jnp.ones(half, F32), jnp.ones(half, F32)]), LANES // HEAD_DIM)
    ang = jnp.arange(seq_len, dtype=F32)[:, None] * inv_lanes[None, :]
    return jnp.cos(ang), jnp.sin(ang) * sign


def _attn_tiles(seq_len):
    tq = 256 if seq_len >= 8192 else _pick(seq_len, 512)
    tk = _pick(seq_len // 2, 512)
    n_blocks = seq_len // tk
    return tq, tk, n_blocks if n_blocks % 2 == 0 and n_blocks <= 32 else 2


def _pick(n, target):
    t = min(n, target)
    assert n % t == 0, (n, t)
    return t


def kernel(x_prompt, x_sample, norm1_g, w_in, q_norm_g, k_norm_g, lam_q1, lam_k1, lam_q2, lam_k2, subln_g,
           mu_prev, mu_next, w0, w_up, a0, a_up, g_up, k_k, k_a, r_k, ln_x_g, ln_x_b, w_out, norm2_g,
           w_ff1, w_ff2):
    l = 0
    w_in_b = w_in[l].astype(BF16)
    w_out_b = w_out[l].astype(BF16)
    w1_b = w_ff1[l].astype(BF16)
    w2_b = w_ff2[l].astype(BF16)
    g1 = norm1_g[l][None, :]
    g2 = norm2_g[l][None, :]
    qg = jnp.tile(q_norm_g[l], DA_WIDTH // HEAD_DIM)[None, :]
    kg = jnp.tile(k_norm_g[l], DA_WIDTH // HEAD_DIM)[None, :]
    lam_p = jnp.stack([lam_q1[l], lam_k1[l], lam_q2[l], lam_k2[l]])
    sg = subln_g[l][None, :]
    mu = jnp.stack([mu_prev[l], mu_next[l]])
    rows = [w0[l, 0], w0[l, 1], a0[l, 0], a0[l, 1], k_k[l], k_a[l], r_k[l].reshape(-1), ln_x_g[l], ln_x_b[l]]
    pvec = jnp.zeros((PV_ROWS, RW_WIDTH), F32).at[:len(rows)].set(jnp.stack(rows))
    zpad = jnp.zeros((2, DECAY_LORA, RW_WIDTH), F32)
    wup = jnp.concatenate([w_up[l], zpad], axis=1).astype(BF16)
    aup = jnp.concatenate([zpad, a_up[l]], axis=1).astype(BF16)
    gup = g_up[l].astype(BF16)
    ch = jnp.arange(SEG_WIDTH) // HEAD_DIM
    seg = (ch[:, None] == ch[None, :]).astype(BF16)
    t = jnp.arange(CHUNK)
    tri = jnp.stack([t[None, :] <= t[:, None], t[None, :] >= t[:, None]]).astype(BF16)

    cos_t, sin_t = _rope_tables(max(x_prompt.shape[1], x_sample.shape[1]))

    def run(x):
        S = x.shape[1]
        tm = _pick(S, TOKEN_TILE)
        qt, k, vt, z_rw = _in_proj(x, g1, w_in_b, qg, kg, cos_t, sin_t, seg, mu, tm)
        tq, tk, per_iter = _attn_tiles(S)
        o_da = _diff_attn(qt, k, vt, lam_p, sg, tq, tk, per_iter)
        y0, y1, bon0, bon1, g = _rwkv_scan(z_rw, pvec, wup, aup, gup, seg, tri, _pick(S, SCAN_TILE))
        return _out_ffn(x, o_da, y0, y1, bon0, bon1, g, pvec, seg, w_out_b, g2, w1_b, w2_b, tm)

    return (run(x_prompt), run(x_sample))
```
